```python
import math
import jax, jax.numpy as jnp
from jax import lax
import numpy as np

D_MODEL = 4096
BATCH = 8
SEQ = 2048
DEPTH = 2
DEC_BATCH = 8
DEC_SEQ = 64
PAST_LEN = 2048

CHUNK = 64
Q_BLOCK = 128
D_HEAD = 128
MIX_A = D_MODEL // 2
MIX_B = D_MODEL - MIX_A
H_A = MIX_A // (2 * D_HEAD)
H_B = MIX_B // D_HEAD
QKV_WIDTH = 3 * MIX_A + 3 * MIX_B
D_FF = 4 * D_MODEL
CONV_WIDTH = 31
ROPE_THETA = 10000.0
NORM_EPS = 1e-6
SUBLN_EPS = 1e-5
N_ATTN = (DEPTH + 1) // 2
N_CONV = DEPTH // 2
NEG_INF = -1e30

kernel_name = 'hybrid_stream_diffattn_stickbreak_conformer_step'


def rmsnorm(x, g, eps=NORM_EPS):
    xf = x.astype(jnp.float32)
    y = xf * lax.rsqrt(jnp.mean(xf * xf, axis=-1, keepdims=True) + eps)
    return (y * g.astype(jnp.float32)).astype(x.dtype)


def layernorm(x, g, b, eps=NORM_EPS):
    xf = x.astype(jnp.float32)
    xc = xf - jnp.mean(xf, axis=-1, keepdims=True)
    var = jnp.mean(xc * xc, axis=-1, keepdims=True)
    return (xc * lax.rsqrt(var + eps) * g.astype(jnp.float32) + b.astype(jnp.float32)).astype(x.dtype)


def rope(x, pos):
    half = D_HEAD // 2
    inv_freq = jnp.power(ROPE_THETA, -jnp.arange(half, dtype=jnp.float32) / half)
    ang = pos.astype(jnp.float32)[:, None] * inv_freq[None, :]
    cos = jnp.cos(ang)[None, :, None, :]
    sin = jnp.sin(ang)[None, :, None, :]
    xf = x.astype(jnp.float32)
    x1, x2 = xf[..., :half], xf[..., half:]
    return jnp.concatenate([x1 * cos - x2 * sin, x2 * cos + x1 * sin], axis=-1).astype(x.dtype)


def sweep_query_blocks(fn, q, q_pos):
    B, S = q.shape[0], q.shape[1]
    nb = S // Q_BLOCK
    qb = jnp.moveaxis(q.reshape((B, nb, Q_BLOCK) + q.shape[2:]), 1, 0)
    pb = q_pos.reshape(nb, Q_BLOCK)
    out = lax.map(lambda args: fn(args[0], args[1]), (qb, pb))
    return jnp.moveaxis(out, 0, 1).reshape(B, S, -1)


def diff_attn_block(q, k, v, q_pos, k_pos, lam, lam_init, subln_g):
    B, Tq = q.shape[0], q.shape[1]
    Tk = k.shape[1]
    s = jnp.einsum('bqhd,bkhd->bhqk', q, k, preferred_element_type=jnp.float32) / math.sqrt(D_HEAD)
    visible = (k_pos[None, :] // CHUNK) <= (q_pos[:, None] // CHUNK)
    s = jnp.where(visible[None, None], s, NEG_INF)
    p = jax.nn.softmax(s, axis=-1).reshape(B, H_A, 2, Tq, Tk)
    w = p[:, :, 0] - lam * p[:, :, 1]
    o = jnp.einsum('bhqk,bkhe->bqhe', w.astype(v.dtype), v)
    o = rmsnorm(o, subln_g, SUBLN_EPS) * (1.0 - lam_init)
    return o.reshape(B, Tq, MIX_A)


def stick_breaking_block(q, k, v, q_pos, k_pos):
    B, Tq = q.shape[0], q.shape[1]
    z = jnp.einsum('bqhd,bkhd->bhqk', q, k, preferred_element_type=jnp.float32) / math.sqrt(D_HEAD)
    before = (k_pos[None, :] < q_pos[:, None])[None, None]
    log_beta = jax.nn.log_sigmoid(z)
    log_rest = jnp.where(before, jax.nn.log_sigmoid(-z), 0.0)
    rev = lax.cumsum(log_rest, axis=3, reverse=True)
    after = jnp.concatenate([rev[..., 1:], jnp.zeros_like(rev[..., :1])], axis=-1)
    a = jnp.where(before, jnp.exp(log_beta + after), 0.0)
    o = jnp.einsum('bhqk,bkhd->bqhd', a.astype(v.dtype), v)
    return o.reshape(B, Tq, MIX_B)


def parallel_attn_mixer(h, pos, cache, w_in, w_out, lq1, lk1, lq2, lk2, subln_g, layer):
    B, T = h.shape[0], h.shape[1]
    proj = h @ w_in
    qa, ka, va, qb, kb, vb = jnp.split(
        proj, [MIX_A, 2 * MIX_A, 3 * MIX_A, 3 * MIX_A + MIX_B, 3 * MIX_A + 2 * MIX_B], axis=-1)
    qa = rope(qa.reshape(B, T, 2 * H_A, D_HEAD), pos)
    ka = rope(ka.reshape(B, T, 2 * H_A, D_HEAD), pos)
    va = va.reshape(B, T, H_A, 2 * D_HEAD)
    qb = qb.reshape(B, T, H_B, D_HEAD)
    kb = kb.reshape(B, T, H_B, D_HEAD)
    vb = vb.reshape(B, T, H_B, D_HEAD)
    lam_init = 0.8 - 0.6 * math.exp(-0.3 * layer)
    lam = (jnp.exp(jnp.sum(lq1.astype(jnp.float32) * lk1.astype(jnp.float32)))
           - jnp.exp(jnp.sum(lq2.astype(jnp.float32) * lk2.astype(jnp.float32))) + lam_init)
    if cache is None:
        ka_all, va_all, kb_all, vb_all, k_pos = ka, va, kb, vb, pos
    else:
        ck_a, cv_a, ck_b, cv_b = cache
        past_len = ck_a.shape[1]
        ka_all = jnp.concatenate([ck_a, ka], axis=1)
        va_all = jnp.concatenate([cv_a, va], axis=1)
        kb_all = jnp.concatenate([ck_b, kb], axis=1)
        vb_all = jnp.concatenate([cv_b, vb], axis=1)
        k_pos = jnp.concatenate([jnp.arange(past_len), pos])
    fa = lambda qblk, pblk: diff_attn_block(qblk, ka_all, va_all, pblk, k_pos, lam, lam_init, subln_g)
    fb = lambda qblk, pblk: stick_breaking_block(qblk, kb_all, vb_all, pblk, k_pos)
    if cache is None:
        o_a = sweep_query_blocks(fa, qa, pos)
        o_b = sweep_query_blocks(fb, qb, pos)
    else:
        o_a = fa(qa, pos)
        o_b = fb(qb, pos)
    out = jnp.concatenate([o_a, o_b], axis=-1) @ w_out
    return out, (ka, va, kb, vb)


def conv_module(h, state, pw1, dw, dw_b, ln_g, ln_b, pw2):
    ab = h @ pw1
    u = ab[..., :D_MODEL] * jax.nn.sigmoid(ab[..., D_MODEL:])
    if state is None:
        state = jnp.zeros((u.shape[0], CONV_WIDTH - 1, D_MODEL), u.dtype)
    up = jnp.concatenate([state.astype(u.dtype), u], axis=1)
    y = lax.conv_general_dilated(up, dw[:, None, :].astype(u.dtype), (1,), 'VALID',
                                 dimension_numbers=('NWC', 'WIO', 'NWC'),
                                 feature_group_count=D_MODEL) + dw_b
    y = jax.nn.silu(layernorm(y, ln_g, ln_b))
    return y @ pw2, up[:, -(CONV_WIDTH - 1):]


def squared_relu_mlp(h, up, down):
    return jnp.square(jax.nn.relu(h @ up)) @ down


def run_trunk(x, c, past, p):
    B, T = x.shape[0], x.shape[1]
    if past is None:
        pos = jnp.arange(T)
    else:
        pos = past['attn'][0].shape[2] + jnp.arange(T)
    new_attn, new_conv = [], []
    cs = jax.nn.silu(c)
    for layer in range(DEPTH):
        mod = cs @ p['w_mod'][layer] + p['b_mod'][layer]
        sh_m, sc_m, g_m, sh_f, sc_f, g_f = [m[:, None, :] for m in jnp.split(mod, 6, axis=-1)]
        h = rmsnorm(x, p['norm_mix'][layer]) * (1.0 + sc_m) + sh_m
        i = layer // 2
        if layer % 2 == 0:
            cache = None if past is None else tuple(a[i] for a in past['attn'])
            out, rows = parallel_attn_mixer(h, pos, cache, p['w_attn_in'][i], p['w_attn_out'][i],
                                            p['lambda_q1'][i], p['lambda_k1'][i], p['lambda_q2'][i],
                                            p['lambda_k2'][i], p['diff_subln_g'][i], layer)
            new_attn.append(rows)
        else:
            st = None if past is None else past['conv'][i]
            out, st_new = conv_module(h, st, p['conv_pw1'][i], p['conv_dw'][i], p['conv_dw_b'][i],
                                      p['conv_ln_g'][i], p['conv_ln_b'][i], p['conv_pw2'][i])
            new_conv.append(st_new)
        x = x + g_m * out
        h = rmsnorm(x, p['norm_mlp'][layer]) * (1.0 + sc_f) + sh_f
        x = x + g_f * squared_relu_mlp(h, p['mlp_up'][layer], p['mlp_down'][layer])
    y = rmsnorm(x, p['final_g'])
    k_diff = jnp.stack([r[0] for r in new_attn])
    v_diff = jnp.stack([r[1] for r in new_attn])
    k_sb = jnp.stack([r[2] for r in new_attn])
    v_sb = jnp.stack([r[3] for r in new_attn])
    conv_state = jnp.stack(new_conv)
    return y, k_diff, v_diff, k_sb, v_sb, conv_state


def setup_inputs(seed: int = 0) -> dict:
    key = jax.random.key(seed)
    ks = jax.random.split(key, 32)
    f32 = jnp.float32
    def nrm(i, shape, scale):
        return jax.random.normal(ks[i], shape, f32) * scale
    d = D_MODEL
    return {
        'x_prompt': nrm(0, (BATCH, SEQ, d), 1.0),
        'x_sample': nrm(1, (DEC_BATCH, DEC_SEQ, d), 1.0),
        'c_prompt': nrm(2, (BATCH, d), 1.0),
        'c_sample': nrm(3, (DEC_BATCH, d), 1.0),
        'cache_k_diff': nrm(4, (N_ATTN, DEC_BATCH, PAST_LEN, 2 * H_A, D_HEAD), 1.0),
        'cache_v_diff': nrm(5, (N_ATTN, DEC_BATCH, PAST_LEN, H_A, 2 * D_HEAD), 1.0),
        'cache_k_sb': nrm(6, (N_ATTN, DEC_BATCH, PAST_LEN, H_B, D_HEAD), 1.0),
        'cache_v_sb': nrm(7, (N_ATTN, DEC_BATCH, PAST_LEN, H_B, D_HEAD), 1.0),
        'state_conv': nrm(8, (N_CONV, DEC_BATCH, CONV_WIDTH - 1, d), 0.5),
        'w_mod': nrm(9, (DEPTH, d, 6 * d), 0.5 * d ** -0.5),
        'b_mod': nrm(10, (DEPTH, 6 * d), 0.02),
        'norm_mix': 1.0 + nrm(11, (DEPTH, d), 0.02),
        'norm_mlp': 1.0 + nrm(12, (DEPTH, d), 0.02),
        'w_attn_in': nrm(13, (N_ATTN, d, QKV_WIDTH), d ** -0.5),
        'w_attn_out': nrm(14, (N_ATTN, MIX_A + MIX_B, d), (MIX_A + MIX_B) ** -0.5),
        'lambda_q1': nrm(15, (N_ATTN, D_HEAD), 0.1),
        'lambda_k1': nrm(16, (N_ATTN, D_HEAD), 0.1),
        'lambda_q2': nrm(17, (N_ATTN, D_HEAD), 0.1),
        'lambda_k2': nrm(18, (N_ATTN, D_HEAD), 0.1),
        'diff_subln_g': 1.0 + nrm(19, (N_ATTN, 2 * D_HEAD), 0.02),
        'conv_pw1': nrm(20, (N_CONV, d, 2 * d), d ** -0.5),
        'conv_dw': nrm(21, (N_CONV, CONV_WIDTH, d), CONV_WIDTH ** -0.5),
        'conv_dw_b': nrm(22, (N_CONV, d), 0.02),
        'conv_ln_g': 1.0 + nrm(23, (N_CONV, d), 0.02),
        'conv_ln_b': nrm(24, (N_CONV, d), 0.02),
        'conv_pw2': nrm(25, (N_CONV, d, d), d ** -0.5),
        'mlp_up': nrm(26, (DEPTH, d, D_FF), d ** -0.5),
        'mlp_down': nrm(27, (DEPTH, D_FF, d), D_FF ** -0.5),
        'final_g': 1.0 + nrm(28, (d,), 0.02),
    }


def reference(x_prompt, x_sample, c_prompt, c_sample, cache_k_diff, cache_v_diff, cache_k_sb, cache_v_sb,
              state_conv, w_mod, b_mod, norm_mix, norm_mlp, w_attn_in, w_attn_out, lambda_q1, lambda_k1,
              lambda_q2, lambda_k2, diff_subln_g, conv_pw1, conv_dw, conv_dw_b, conv_ln_g, conv_ln_b, conv_pw2,
              mlp_up, mlp_down, final_g):
    p = dict(w_mod=w_mod, b_mod=b_mod, norm_mix=norm_mix, norm_mlp=norm_mlp, w_attn_in=w_attn_in,
             w_attn_out=w_attn_out, lambda_q1=lambda_q1, lambda_k1=lambda_k1, lambda_q2=lambda_q2,
             lambda_k2=lambda_k2, diff_subln_g=diff_subln_g, conv_pw1=conv_pw1, conv_dw=conv_dw,
             conv_dw_b=conv_dw_b, conv_ln_g=conv_ln_g, conv_ln_b=conv_ln_b, conv_pw2=conv_pw2,
             mlp_up=mlp_up, mlp_down=mlp_down, final_g=final_g)
    y_prompt, kd_p, vd_p, ks_p, vs_p, conv_p = run_trunk(x_prompt, c_prompt, None, p)
    past = dict(attn=(cache_k_diff, cache_v_diff, cache_k_sb, cache_v_sb), conv=state_conv)
    y_sample, kd_s, vd_s, ks_s, vs_s, conv_s = run_trunk(x_sample, c_sample, past, p)
    return (y_prompt, y_sample, kd_p, vd_p, ks_p, vs_p, conv_p, kd_s, vd_s, ks_s, vs_s, conv_s)
```

```python
import functools
import math

import jax
import jax.numpy as jnp
import numpy as np
from jax import lax
from jax.experimental import pallas as pl
from jax.experimental.pallas import tpu as pltpu

CHUNK = 64
D_HEAD = 128
CONV_WIDTH = 31
ROPE_THETA = 10000.0
NORM_EPS = 1e-6
SUBLN_EPS = 1e-5
NEG_INF = -1e30

F32 = jnp.float32
BF16 = jnp.bfloat16

V7X_VMEM_LIMIT_BYTES = 60000 * 1024
SUBLANE = 8
CONV_HALO = -(-(CONV_WIDTH - 1) // SUBLANE) * SUBLANE


def _params(*sem):
    return pltpu.CompilerParams(dimension_semantics=sem, vmem_limit_bytes=V7X_VMEM_LIMIT_BYTES)


def _tile(n, pref):
    t = min(n, pref)
    while n % t:
        t -= 1
    return t


def _dot(a, b):
    return jnp.dot(a, b, preferred_element_type=F32)


def _dot_nt(a, b):
    return lax.dot_general(a, b, (((1,), (1,)), ((), ())), preferred_element_type=F32)


def _rows_like(v, t, tm):
    if t % tm == 0:
        return v[:, None, :]
    b, d = v.shape
    assert tm % t == 0 and (b * t) % tm == 0
    return jnp.repeat(v, t, axis=0).reshape(b * t // tm, tm, d)


def _group_index(arr, m, tm):
    tiles_per_group = (m // tm) // arr.shape[0]
    return tiles_per_group


def _mod_kernel(c_ref, w_ref, b_ref, o_ref):
    c = c_ref[...]
    cs = (c * jax.nn.sigmoid(c)).astype(BF16)
    o_ref[...] = _dot(cs, w_ref[...].astype(BF16)) + b_ref[...]


def _modulation(c_all, w_mod, b_mod):
    nl, d, n = w_mod.shape
    r = c_all.shape[0]
    tn = _tile(n, 512)
    return pl.pallas_call(
        _mod_kernel,
        grid=(nl, n // tn),
        in_specs=[pl.BlockSpec((r, d), lambda l, j: (0, 0)),
                  pl.BlockSpec((None, d, tn), lambda l, j: (l, 0, j)),
                  pl.BlockSpec((None, 1, tn), lambda l, j: (l, 0, j))],
        out_specs=pl.BlockSpec((None, r, tn), lambda l, j: (l, 0, j)),
        out_shape=jax.ShapeDtypeStruct((nl, r, n), F32),
        compiler_params=_params("arbitrary", "arbitrary"),
        name="modulation",
    )(c_all, w_mod, b_mod.reshape(nl, 1, n))


def _norm_mod_kernel(x_ref, g_ref, sc_ref, sh_ref, o_ref):
    x = x_ref[...]
    y = x * lax.rsqrt(jnp.mean(x * x, axis=-1, keepdims=True) + NORM_EPS) * g_ref[...]
    o_ref[...] = (y * (1.0 + sc_ref[...]) + sh_ref[...]).astype(o_ref.dtype)


def _norm_mod(x, g, sc, sh, t, tm):
    m, d = x.shape
    sc3, sh3 = _rows_like(sc, t, tm), _rows_like(sh, t, tm)
    tpg = _group_index(sc3, m, tm)
    r = sc3.shape[1]
    return pl.pallas_call(
        _norm_mod_kernel,
        grid=(m // tm,),
        in_specs=[pl.BlockSpec((tm, d), lambda i: (i, 0)),
                  pl.BlockSpec((1, d), lambda i: (0, 0)),
                  pl.BlockSpec((None, r, d), lambda i: (i // tpg, 0, 0)),
                  pl.BlockSpec((None, r, d), lambda i: (i // tpg, 0, 0))],
        out_specs=pl.BlockSpec((tm, d), lambda i: (i, 0)),
        out_shape=jax.ShapeDtypeStruct((m, d), BF16),
        compiler_params=_params("arbitrary"),
        name="norm_mod",
    )(x, g.reshape(1, d), sc3, sh3)


def _proj_kernel(*refs, rope):
    if rope:
        x_ref, w_ref, cos_ref, sin_ref, o_ref = refs
    else:
        x_ref, w_ref, o_ref = refs
    acc = _dot(x_ref[...], w_ref[...])
    if rope:
        cos, sin = cos_ref[...], sin_ref[...]
        for h in range(acc.shape[1] // D_HEAD):
            sl = slice(h * D_HEAD, (h + 1) * D_HEAD)
            xh = acc[:, sl]
            o_ref[:, sl] = (xh * cos + pltpu.roll(xh, D_HEAD // 2, 1) * sin).astype(o_ref.dtype)
    else:
        o_ref[...] = acc.astype(o_ref.dtype)


def _proj(x, w, col_off, n, out_dtype, tm, rope_tabs=None, t=None):
    m, k = x.shape
    tn = _tile(n, 1024)
    assert col_off % tn == 0
    joff = col_off // tn
    in_specs = [pl.BlockSpec((tm, k), lambda i, j: (i, 0)),
                pl.BlockSpec((k, tn), lambda i, j: (0, j + joff))]
    args = [x, w]
    if rope_tabs is not None:
        cos, sin = rope_tabs
        if t % tm == 0:
            nblk = t // tm
            tab_map = lambda i, j: (i % nblk, 0)
        else:
            assert tm % t == 0
            cos, sin = jnp.tile(cos, (tm // t, 1)), jnp.tile(sin, (tm // t, 1))
            tab_map = lambda i, j: (0, 0)
        in_specs += [pl.BlockSpec((tm, D_HEAD), tab_map), pl.BlockSpec((tm, D_HEAD), tab_map)]
        args += [cos, sin]
    return pl.pallas_call(
        functools.partial(_proj_kernel, rope=rope_tabs is not None),
        grid=(m // tm, n // tn),
        in_specs=in_specs,
        out_specs=pl.BlockSpec((tm, tn), lambda i, j: (i, j)),
        out_shape=jax.ShapeDtypeStruct((m, n), out_dtype),
        compiler_params=_params("arbitrary", "arbitrary"),
        name="proj_rope" if rope_tabs is not None else "proj",
    )(*args)


class _Masks:
    def __init__(self):
        self._cache = {}

    def _get(self, kind, q0, tq, k0, n):
        qp = np.arange(q0, q0 + tq)[:, None]
        kp = np.arange(k0, k0 + n)[None, :]
        vis = (kp // CHUNK <= qp // CHUNK) if kind == "chunk" else (kp < qp)
        if vis.all():
            return None
        assert vis.any()
        aligned = q0 % CHUNK == 0 and k0 % CHUNK == 0
        key = (kind, q0 - k0, tq, n) if (aligned or kind == "before") else (kind, q0, k0, tq, n)
        if key not in self._cache:
            row = lax.broadcasted_iota(jnp.int32, (tq, n), 0) + q0
            col = lax.broadcasted_iota(jnp.int32, (tq, n), 1) + k0
            self._cache[key] = (col // CHUNK <= row // CHUNK) if kind == "chunk" else (col < row)
        return self._cache[key]

    def chunk(self, q0, tq, k0, n):
        return self._get("chunk", q0, tq, k0, n)

    def before(self, q0, tq, k0, n):
        return self._get("before", q0, tq, k0, n)


def _lambda(lq1, lk1, lq2, lk2, lam_init):
    return (jnp.exp(jnp.sum(lq1[...] * lk1[...], keepdims=True))
            - jnp.exp(jnp.sum(lq2[...] * lk2[...], keepdims=True)) + lam_init)


def _diff_rows(q0, q1, segs, lam, subln_g, out_scale):
    scale = 1.0 / math.sqrt(D_HEAD)
    s0s, s1s = [], []
    for k0, k1, _, mask in segs:
        s0 = _dot_nt(q0, k0) * scale
        s1 = _dot_nt(q1, k1) * scale
        if mask is not None:
            s0 = jnp.where(mask, s0, NEG_INF)
            s1 = jnp.where(mask, s1, NEG_INF)
        s0s.append(s0)
        s1s.append(s1)
    m0 = functools.reduce(jnp.maximum, [jnp.max(s, axis=1, keepdims=True) for s in s0s])
    m1 = functools.reduce(jnp.maximum, [jnp.max(s, axis=1, keepdims=True) for s in s1s])
    p0s = [jnp.exp(s - m0) for s in s0s]
    p1s = [jnp.exp(s - m1) for s in s1s]
    l0 = functools.reduce(jnp.add, [jnp.sum(p, axis=1, keepdims=True) for p in p0s])
    l1 = functools.reduce(jnp.add, [jnp.sum(p, axis=1, keepdims=True) for p in p1s])
    inv0 = 1.0 / l0
    inv1 = lam / l1
    o = None
    for (_, _, v, _), p0, p1 in zip(segs, p0s, p1s):
        part = _dot((p0 * inv0 - p1 * inv1).astype(BF16), v)
        o = part if o is None else o + part
    o = o * lax.rsqrt(jnp.mean(o * o, axis=-1, keepdims=True) + SUBLN_EPS) * subln_g
    return o * out_scale


def _tri(n):
    j = lax.broadcasted_iota(jnp.int32, (n, n), 0)
    s = lax.broadcasted_iota(jnp.int32, (n, n), 1)
    return (j >= s).astype(BF16)


def _sb_segment(q, k, v, mask, tri, carry, o):
    z = _dot_nt(q, k) * (1.0 / math.sqrt(D_HEAD))
    t = jnp.log1p(jnp.exp(-jnp.abs(z)))
    log_beta = jnp.minimum(z, 0.0) - t
    log_rest = -jnp.maximum(z, 0.0) - t
    if mask is not None:
        log_rest = jnp.where(mask, log_rest, 0.0)
    hi = log_rest.astype(BF16)
    lo = (log_rest - hi.astype(F32)).astype(BF16)
    incl = _dot(hi, tri) + _dot(lo, tri)
    a = jnp.exp(log_beta + (incl - log_rest) + carry)
    if mask is not None:
        a = jnp.where(mask, a, 0.0)
    return carry + incl[:, 0:1], o + _dot(a.astype(BF16), v)


def _attn_a_self_kernel(q_ref, k_ref, v_ref, lq1, lk1, lq2, lk2, g_ref, o_ref, kb, vb, *, tq, lam_init):
    s_len = q_ref.shape[0]
    kb[...] = k_ref[...].astype(BF16)
    vb[...] = v_ref[...].astype(BF16)
    lam = _lambda(lq1, lk1, lq2, lk2, lam_init)
    g = g_ref[...]
    masks = _Masks()
    for qi in range(s_len // tq):
        r0 = qi * tq
        rows = slice(r0, r0 + tq)
        segs = []
        if qi > 0:
            segs.append((kb[0:r0, :D_HEAD], kb[0:r0, D_HEAD:], vb[0:r0, :], masks.chunk(r0, tq, 0, r0)))
        segs.append((kb[rows, :D_HEAD], kb[rows, D_HEAD:], vb[rows, :], masks.chunk(r0, tq, r0, tq)))
        o = _diff_rows(q_ref[rows, :D_HEAD], q_ref[rows, D_HEAD:], segs, lam, g, 1.0 - lam_init)
        o_ref[rows, :] = o.astype(o_ref.dtype)


def _attn_a_self(q, k, v, lams, subln_g, b, s_len, lam_init):
    m, width = q.shape
    hw = 2 * D_HEAD
    tq = _tile(s_len, 256)
    assert tq % CHUNK == 0
    blk = pl.BlockSpec((s_len, hw), lambda i, h: (i, h))
    vec = pl.BlockSpec((1, D_HEAD), lambda i, h: (0, 0))
    return pl.pallas_call(
        functools.partial(_attn_a_self_kernel, tq=tq, lam_init=lam_init),
        grid=(b, width // hw),
        in_specs=[blk, blk, blk, vec, vec, vec, vec, pl.BlockSpec((1, hw), lambda i, h: (0, 0))],
        out_specs=blk,
        out_shape=jax.ShapeDtypeStruct((m, width), BF16),
        scratch_shapes=[pltpu.VMEM((s_len, hw), BF16), pltpu.VMEM((s_len, hw), BF16)],
        compiler_params=_params("arbitrary", "arbitrary"),
        name="diff_attn_self",
    )(q, k, v, *lams, subln_g)


def _attn_b_self_kernel(q_ref, k_ref, v_ref, o_ref, kb, vb, *, tq):
    s_len = q_ref.shape[0]
    kb[...] = k_ref[...].astype(BF16)
    vb[...] = v_ref[...].astype(BF16)
    tri = _tri(tq)
    masks = _Masks()
    for qi in range(s_len // tq):
        r0 = qi * tq
        rows = slice(r0, r0 + tq)
        q = q_ref[rows, :]
        carry = jnp.zeros((tq, 1), F32)
        o = jnp.zeros((tq, D_HEAD), F32)
        carry, o = _sb_segment(q, kb[rows, :], vb[rows, :], masks.before(r0, tq, r0, tq), tri, carry, o)

        def body(it, co, qi=qi, q=q):
            k0 = pl.multiple_of((qi - 1 - it) * tq, tq)
            return _sb_segment(q, kb[pl.ds(k0, tq), :], vb[pl.ds(k0, tq), :], None, tri, *co)

        if qi > 0:
            carry, o = lax.fori_loop(0, qi, body, (carry, o))
        o_ref[rows, :] = o.astype(o_ref.dtype)


def _attn_b_self(q, k, v, b, s_len):
    m, width = q.shape
    tq = _tile(s_len, 256)
    blk = pl.BlockSpec((s_len, D_HEAD), lambda i, h: (i, h))
    return pl.pallas_call(
        functools.partial(_attn_b_self_kernel, tq=tq),
        grid=(b, width // D_HEAD),
        in_specs=[blk, blk, blk],
        out_specs=blk,
        out_shape=jax.ShapeDtypeStruct((m, width), BF16),
        scratch_shapes=[pltpu.VMEM((s_len, D_HEAD), BF16), pltpu.VMEM((s_len, D_HEAD), BF16)],
        compiler_params=_params("arbitrary", "arbitrary"),
        name="stick_attn_self",
    )(q, k, v)


def _attn_a_cached_kernel(q_ref, kc_ref, vc_ref, kn_ref, vn_ref, lq1, lk1, lq2, lk2, g_ref, o_ref, *, lam_init):
    t = q_ref.shape[0]
    p = kc_ref.shape[0]
    kc = kc_ref[...].astype(BF16)
    vc = vc_ref[...].astype(BF16)
    kn = kn_ref[...].astype(BF16)
    vn = vn_ref[...].astype(BF16)
    lam = _lambda(lq1, lk1, lq2, lk2, lam_init)
    masks = _Masks()
    segs = [(kc[:, :D_HEAD], kc[:, D_HEAD:], vc, masks.chunk(p, t, 0, p)),
            (kn[:, :D_HEAD], kn[:, D_HEAD:], vn, masks.chunk(p, t, p, t))]
    o = _diff_rows(q_ref[:, :D_HEAD], q_ref[:, D_HEAD:], segs, lam, g_ref[...], 1.0 - lam_init)
    o_ref[...] = o.astype(o_ref.dtype)


def _attn_a_cached(q, kc, vc, kn, vn, lams, subln_g, b, t, lam_init):
    m, width = q.shape
    p = kc.shape[1]
    hw = 2 * D_HEAD
    blk = pl.BlockSpec((t, hw), lambda i, h: (i, h))
    cblk = pl.BlockSpec((None, p, hw), lambda i, h: (i, 0, h))
    vec = pl.BlockSpec((1, D_HEAD), lambda i, h: (0, 0))
    return pl.pallas_call(
        functools.partial(_attn_a_cached_kernel, lam_init=lam_init),
        grid=(b, width // hw),
        in_specs=[blk, cblk, cblk, blk, blk, vec, vec, vec, vec, pl.BlockSpec((1, hw), lambda i, h: (0, 0))],
        out_specs=blk,
        out_shape=jax.ShapeDtypeStruct((m, width), BF16),
        compiler_params=_params("arbitrary", "arbitrary"),
        name="diff_attn_cached",
    )(q, kc, vc, kn, vn, *lams, subln_g)


def _attn_b_cached_kernel(q_ref, kc_ref, vc_ref, kn_ref, vn_ref, o_ref, kb, vb, *, tk):
    t = q_ref.shape[0]
    p = kc_ref.shape[0]
    kb[...] = kc_ref[...].astype(BF16)
    vb[...] = vc_ref[...].astype(BF16)
    masks = _Masks()
    q = q_ref[...]
    carry = jnp.zeros((t, 1), F32)
    o = jnp.zeros((t, D_HEAD), F32)
    carry, o = _sb_segment(q, kn_ref[...].astype(BF16), vn_ref[...].astype(BF16),
                           masks.before(p, t, p, t), _tri(t), carry, o)
    assert masks.before(p, t, 0, p) is None
    tri = _tri(tk)
    nblk = p // tk

    def body(it, co):
        k0 = pl.multiple_of((nblk - 1 - it) * tk, tk)
        return _sb_segment(q, kb[pl.ds(k0, tk), :], vb[pl.ds(k0, tk), :], None, tri, *co)

    carry, o = lax.fori_loop(0, nblk, body, (carry, o))
    o_ref[...] = o.astype(o_ref.dtype)


def _attn_b_cached(q, kc, vc, kn, vn, b, t):
    m, width = q.shape
    p = kc.shape[1]
    tk = _tile(p, 256)
    blk = pl.BlockSpec((t, D_HEAD), lambda i, h: (i, h))
    cblk = pl.BlockSpec((None, p, D_HEAD), lambda i, h: (i, 0, h))
    return pl.pallas_call(
        functools.partial(_attn_b_cached_kernel, tk=tk),
        grid=(b, width // D_HEAD),
        in_specs=[blk, cblk, cblk, blk, blk],
        out_specs=blk,
        out_shape=jax.ShapeDtypeStruct((m, width), BF16),
        scratch_shapes=[pltpu.VMEM((p, D_HEAD), BF16), pltpu.VMEM((p, D_HEAD), BF16)],
        compiler_params=_params("arbitrary", "arbitrary"),
        name="stick_attn_cached",
    )(q, kc, vc, kn, vn)


def _res_mm_kernel(*refs, nx):
    xs, ws = refs[:nx], refs[nx:2 * nx]
    xres_ref, gate_ref, o_ref = refs[2 * nx:]
    acc = _dot(xs[0][...], ws[0][...])
    for x_ref, w_ref in zip(xs[1:], ws[1:]):
        acc = acc + _dot(x_ref[...], w_ref[...])
    o_ref[...] = xres_ref[...] + gate_ref[...] * acc


def _res_matmul(xs, w, xres, gate, t, tm):
    m, n = xres.shape
    kp = xs[0].shape[1]
    nx = len(xs)
    assert all(x.shape == (m, kp) for x in xs) and w.shape == (nx * kp, n)
    tn = _tile(n, 512)
    g3 = _rows_like(gate, t, tm)
    tpg = _group_index(g3, m, tm)
    r = g3.shape[1]
    in_specs = [pl.BlockSpec((tm, kp), lambda i, j: (i, 0)) for _ in xs]
    in_specs += [pl.BlockSpec((kp, tn), functools.partial(lambda i, j, p: (p, j), p=p)) for p in range(nx)]
    in_specs += [pl.BlockSpec((tm, tn), lambda i, j: (i, j)),
                 pl.BlockSpec((None, r, tn), lambda i, j: (i // tpg, 0, j))]
    return pl.pallas_call(
        functools.partial(_res_mm_kernel, nx=nx),
        grid=(m // tm, n // tn),
        in_specs=in_specs,
        out_specs=pl.BlockSpec((tm, tn), lambda i, j: (i, j)),
        out_shape=jax.ShapeDtypeStruct((m, n), F32),
        compiler_params=_params("arbitrary", "arbitrary"),
        name="res_matmul",
    )(*xs, *([w] * nx), xres, g3)


def _mlp_kernel(*refs, final, nchunk):
    if final:
        h_ref, wu_ref, wd_ref, x_ref, g_ref, fg_ref, o_ref = refs
    else:
        h_ref, wu_ref, wd_ref, x_ref, g_ref, o_ref = refs
    f = pl.program_id(1)
    hid = _dot(h_ref[...], wu_ref[...])
    hid = jnp.square(jnp.maximum(hid, 0.0)).astype(BF16)
    d = o_ref.shape[1]
    cols = [slice(c * (d // nchunk), (c + 1) * (d // nchunk)) for c in range(nchunk)]

    @pl.when(f == 0)
    def _():
        for sl in cols:
            o_ref[:, sl] = _dot(hid, wd_ref[:, sl])

    @pl.when(f > 0)
    def _():
        for sl in cols:
            o_ref[:, sl] += _dot(hid, wd_ref[:, sl])

    @pl.when(f == pl.num_programs(1) - 1)
    def _():
        x = x_ref[...] + g_ref[...] * o_ref[...]
        if final:
            x = x * lax.rsqrt(jnp.mean(x * x, axis=-1, keepdims=True) + NORM_EPS) * fg_ref[...]
        o_ref[...] = x


def _mlp(h, wu, wd, xres, gate, t, tm, final_g=None):
    m, d = xres.shape
    dff = wu.shape[1]
    tf = _tile(dff, 512)
    g3 = _rows_like(gate, t, tm)
    tpg = _group_index(g3, m, tm)
    r = g3.shape[1]
    in_specs = [pl.BlockSpec((tm, d), lambda i, f: (i, 0)),
                pl.BlockSpec((d, tf), lambda i, f: (0, f)),
                pl.BlockSpec((tf, d), lambda i, f: (f, 0)),
                pl.BlockSpec((tm, d), lambda i, f: (i, 0), pipeline_mode=pl.Buffered(1)),
                pl.BlockSpec((None, r, d), lambda i, f: (i // tpg, 0, 0))]
    args = [h, wu, wd, xres, g3]
    if final_g is not None:
        in_specs.append(pl.BlockSpec((1, d), lambda i, f: (0, 0)))
        args.append(final_g.reshape(1, d))
    return pl.pallas_call(
        functools.partial(_mlp_kernel, final=final_g is not None, nchunk=max(1, d // 512)),
        grid=(m // tm, dff // tf),
        in_specs=in_specs,
        out_specs=pl.BlockSpec((tm, d), lambda i, f: (i, 0)),
        out_shape=jax.ShapeDtypeStruct((m, d), F32),
        compiler_params=_params("arbitrary", "arbitrary"),
        name="mlp_final" if final_g is not None else "mlp",
    )(*args)


def _glu_kernel(x_ref, wa_ref, wb_ref, o_ref):
    x = x_ref[...]
    o_ref[...] = _dot(x, wa_ref[...]) * jax.nn.sigmoid(_dot(x, wb_ref[...]))


def _glu(x, w, tm):
    m, k = x.shape
    d = w.shape[1] // 2
    tn = _tile(d, 512)
    nj = d // tn
    return pl.pallas_call(
        _glu_kernel,
        grid=(m // tm, nj),
        in_specs=[pl.BlockSpec((tm, k), lambda i, j: (i, 0)),
                  pl.BlockSpec((k, tn), lambda i, j: (0, j)),
                  pl.BlockSpec((k, tn), lambda i, j: (0, j + nj))],
        out_specs=pl.BlockSpec((tm, tn), lambda i, j: (i, j)),
        out_shape=jax.ShapeDtypeStruct((m, d), F32),
        compiler_params=_params("arbitrary", "arbitrary"),
        name="pointwise_glu",
    )(x, w, w)


def _conv_kernel(u_ref, prev_ref, st_ref, dw_ref, dwb_ref, lng_ref, lnb_ref, y_ref, ns_ref, up, yacc,
                 *, ts, lane_chunk, row_chunk):
    ti = pl.program_id(1)
    d = u_ref.shape[1]
    off = CONV_HALO - (CONV_WIDTH - 1)

    @pl.when(ti == 0)
    def _():
        up[0:CONV_HALO, :] = st_ref[...]

    @pl.when(ti > 0)
    def _():
        up[0:CONV_HALO, :] = prev_ref[...]

    up[CONV_HALO:CONV_HALO + ts, :] = u_ref[...]

    def lane_body(c, carry):
        cols = pl.ds(pl.multiple_of(c * lane_chunk, lane_chunk), lane_chunk)
        dw = dw_ref[:, cols]
        bias = dwb_ref[:, cols]
        for r0 in range(0, ts, row_chunk):
            acc = up[r0 + off:r0 + off + row_chunk, cols] * dw[0:1, :]
            for w in range(1, CONV_WIDTH):
                acc = acc + up[r0 + off + w:r0 + off + w + row_chunk, cols] * dw[w:w + 1, :]
            yacc[r0:r0 + row_chunk, cols] = acc + bias
        return carry

    lax.fori_loop(0, d // lane_chunk, lane_body, 0)
    y = yacc[...]
    yc = y - jnp.mean(y, axis=-1, keepdims=True)
    var = jnp.mean(yc * yc, axis=-1, keepdims=True)
    yn = yc * lax.rsqrt(var + NORM_EPS) * lng_ref[...] + lnb_ref[...]
    y_ref[...] = (yn * jax.nn.sigmoid(yn)).astype(y_ref.dtype)

    @pl.when(ti == pl.num_programs(1) - 1)
    def _():
        ns_ref[...] = up[CONV_HALO + ts - (CONV_WIDTH - 1):CONV_HALO + ts, :]


def _conv_ln_swish(u, state, dw, dw_b, ln_g, ln_b, b, s_len):
    m, d = u.shape
    ts = _tile(s_len, 256)
    assert ts % CONV_HALO == 0 and s_len >= CONV_WIDTH - 1
    nt = s_len // ts
    st = jnp.pad(state, ((0, 0), (CONV_HALO - (CONV_WIDTH - 1), 0), (0, 0)))
    hpb = ts // CONV_HALO
    vec = pl.BlockSpec((1, d), lambda i, t: (0, 0))
    y, ns = pl.pallas_call(
        functools.partial(_conv_kernel, ts=ts, lane_chunk=_tile(d, 512), row_chunk=_tile(ts, 32)),
        grid=(b, nt),
        in_specs=[pl.BlockSpec((ts, d), lambda i, t: (i * nt + t, 0)),
                  pl.BlockSpec((CONV_HALO, d), lambda i, t: (jnp.maximum((i * nt + t) * hpb - 1, 0), 0)),
                  pl.BlockSpec((None, CONV_HALO, d), lambda i, t: (i, 0, 0)),
                  pl.BlockSpec((CONV_WIDTH, d), lambda i, t: (0, 0)),
                  vec, vec, vec],
        out_specs=[pl.BlockSpec((ts, d), lambda i, t: (i * nt + t, 0)),
                   pl.BlockSpec((None, CONV_WIDTH - 1, d), lambda i, t: (i, 0, 0))],
        out_shape=[jax.ShapeDtypeStruct((m, d), BF16),
                   jax.ShapeDtypeStruct((b, CONV_WIDTH - 1, d), F32)],
        scratch_shapes=[pltpu.VMEM((CONV_HALO + ts, d), F32), pltpu.VMEM((ts, d), F32)],
        compiler_params=_params("arbitrary", "arbitrary"),
        name="conv_ln_swish",
    )(u, u, st, dw, dw_b.reshape(1, d), ln_g.reshape(1, d), ln_b.reshape(1, d))
    return y, ns


def _rope_tables(pos):
    half = D_HEAD // 2
    inv_freq = jnp.power(ROPE_THETA, -jnp.arange(half, dtype=F32) / half)
    ang = pos.astype(F32)[:, None] * inv_freq[None, :]
    cos, sin = jnp.cos(ang), jnp.sin(ang)
    return jnp.concatenate([cos, cos], axis=-1), jnp.concatenate([-sin, sin], axis=-1)


def _trunk(x3, mod, cache, state, p, wb):
    b, t, d = x3.shape
    m = b * t
    mix_a = d // 2
    mix_b = d - mix_a
    x = x3.reshape(m, d)
    tm = _tile(m, 1024)
    tm_mlp = _tile(m, 512)
    tm_norm = _tile(m, 512) if t % _tile(m, 512) else _tile(t, 256)
    past_len = 0 if cache is None else cache[0].shape[1]
    rope_tabs = _rope_tables(past_len + jnp.arange(t))

    def split_mod(layer):
        return jnp.split(mod[layer], 6, axis=-1)

    sh_m, sc_m, g_m, sh_f, sc_f, g_f = split_mod(0)
    h = _norm_mod(x, p["norm_mix"][0], sc_m, sh_m, t, tm_norm)
    w_in = wb["w_attn_in"]
    qa = _proj(h, w_in, 0, mix_a, BF16, tm, rope_tabs, t)
    ka = _proj(h, w_in, mix_a, mix_a, F32, tm, rope_tabs, t)
    va = _proj(h, w_in, 2 * mix_a, mix_a, F32, tm)
    qb = _proj(h, w_in, 3 * mix_a, mix_b, BF16, tm)
    kb = _proj(h, w_in, 3 * mix_a + mix_b, mix_b, F32, tm)
    vb = _proj(h, w_in, 3 * mix_a + 2 * mix_b, mix_b, F32, tm)
    lams = [p[n][0].reshape(1, D_HEAD) for n in ("lambda_q1", "lambda_k1", "lambda_q2", "lambda_k2")]
    subln_g = p["diff_subln_g"][0].reshape(1, 2 * D_HEAD)
    lam_init = 0.8 - 0.6 * math.exp(-0.3 * 0)
    if cache is None:
        o_a = _attn_a_self(qa, ka, va, lams, subln_g, b, t, lam_init)
        o_b = _attn_b_self(qb, kb, vb, b, t)
    else:
        ck_a, cv_a, ck_b, cv_b = cache
        o_a = _attn_a_cached(qa, ck_a.reshape(b, past_len, mix_a), cv_a.reshape(b, past_len, mix_a),
                             ka, va, lams, subln_g, b, t, lam_init)
        o_b = _attn_b_cached(qb, ck_b.reshape(b, past_len, mix_b), cv_b.reshape(b, past_len, mix_b),
                             kb, vb, b, t)
    x = _res_matmul([o_a, o_b], wb["w_attn_out"], x, g_m, t, tm)
    h = _norm_mod(x, p["norm_mlp"][0], sc_f, sh_f, t, tm_norm)
    x = _mlp(h, wb["mlp_up"][0], wb["mlp_down"][0], x, g_f, t, tm_mlp)

    sh_m, sc_m, g_m, sh_f, sc_f, g_f = split_mod(1)
    h = _norm_mod(x, p["norm_mix"][1], sc_m, sh_m, t, tm_norm)
    u = _glu(h, wb["conv_pw1"], tm)
    if state is None:
        state = jnp.zeros((b, CONV_WIDTH - 1, d), F32)
    y, new_state = _conv_ln_swish(u, state, p["conv_dw"][0], p["conv_dw_b"][0], p["conv_ln_g"][0],
                                  p["conv_ln_b"][0], b, t)
    x = _res_matmul([y], wb["conv_pw2"], x, g_m, t, tm)
    h = _norm_mod(x, p["norm_mlp"][1], sc_f, sh_f, t, tm_norm)
    y_out = _mlp(h, wb["mlp_up"][1], wb["mlp_down"][1], x, g_f, t, tm_mlp, final_g=p["final_g"])

    h_a, h_b = mix_a // (2 * D_HEAD), mix_b // D_HEAD
    return (y_out.reshape(b, t, d),
            ka.reshape(1, b, t, 2 * h_a, D_HEAD), va.reshape(1, b, t, h_a, 2 * D_HEAD),
            kb.reshape(1, b, t, h_b, D_HEAD), vb.reshape(1, b, t, h_b, D_HEAD),
            new_state[None])


def kernel(x_prompt, x_sample, c_prompt, c_sample, cache_k_diff, cache_v_diff, cache_k_sb, cache_v_sb, state_conv, w_mod, b_mod, norm_mix, norm_mlp, w_attn_in, w_attn_out, lambda_q1, lambda_k1, lambda_q2, lambda_k2, diff_subln_g, conv_pw1, conv_dw, conv_dw_b, conv_ln_g, conv_ln_b, conv_pw2, mlp_up, mlp_down, final_g):
    assert w_mod.shape[0] == 2 and w_attn_in.shape[0] == 1 and conv_pw1.shape[0] == 1
    p = dict(norm_mix=norm_mix, norm_mlp=norm_mlp, lambda_q1=lambda_q1, lambda_k1=lambda_k1,
             lambda_q2=lambda_q2, lambda_k2=lambda_k2, diff_subln_g=diff_subln_g, conv_dw=conv_dw,
             conv_dw_b=conv_dw_b, conv_ln_g=conv_ln_g, conv_ln_b=conv_ln_b, final_g=final_g)
    wb = dict(w_attn_in=w_attn_in[0].astype(BF16), w_attn_out=w_attn_out[0].astype(BF16),
              conv_pw1=conv_pw1[0].astype(BF16), conv_pw2=conv_pw2[0].astype(BF16),
              mlp_up=mlp_up.astype(BF16), mlp_down=mlp_down.astype(BF16))
    nb = c_prompt.shape[0]
    mod = _modulation(jnp.concatenate([c_prompt, c_sample], axis=0), w_mod, b_mod)
    out_p = _trunk(x_prompt, mod[:, :nb], None, None, p, wb)
    cache = (cache_k_diff[0], cache_v_diff[0], cache_k_sb[0], cache_v_sb[0])
    out_s = _trunk(x_sample, mod[:, nb:], cache, state_conv[0], p, wb)
    return (out_p[0], out_s[0]) + out_p[1:] + out_s[1:]
```

```python
import functools
import math

import jax
import jax.numpy as jnp
import numpy as np
from jax import lax
from jax.experimental import pallas as pl
from jax.experimental.pallas import tpu as pltpu

CHUNK = 64
D_HEAD = 128
CONV_WIDTH = 31
ROPE_THETA = 10000.0
NORM_EPS = 1e-6
SUBLN_EPS = 1e-5
NEG_INF = -1e30

F32 = jnp.float32
BF16 = jnp.bfloat16

V7X_VMEM_LIMIT_BYTES = 60000 * 1024
SUBLANE = 8
CONV_HALO = -(-(CONV_WIDTH - 1) // SUBLANE) * SUBLANE


def _params(*sem):
    return pltpu.CompilerParams(dimension_semantics=sem, vmem_limit_bytes=V7X_VMEM_LIMIT_BYTES)


def _tile(n, pref):
    t = min(n, pref)
    while n % t:
        t -= 1
    return t


def _dot(a, b):
    return jnp.dot(a, b, preferred_element_type=F32)


def _dot_nt(a, b):
    return lax.dot_general(a, b, (((1,), (1,)), ((), ())), preferred_element_type=F32)


def _stream_vec_spec(t, tm, width, with_col):
    if t % tm == 0:
        spt, tps = 1, t // tm
    else:
        assert tm % t == 0
        spt, tps = tm // t, 1
    if with_col:
        return pl.BlockSpec((spt, 1, width), lambda i, j: (i // tps, 0, j))
    return pl.BlockSpec((spt, 1, width), lambda i, *_: (i // tps, 0, 0))


def _stream_rows(vec_ref, tm):
    spt = vec_ref.shape[0]
    rows = tm // spt
    return [(s, slice(s * rows, (s + 1) * rows)) for s in range(spt)]


def _mod_kernel(c_ref, w_ref, b_ref, o_ref):
    c = c_ref[...]
    cs = (c * jax.nn.sigmoid(c)).astype(BF16)
    o_ref[...] = _dot(cs, w_ref[...].astype(BF16)) + b_ref[...]


def _modulation(c_all, w_mod, b_mod):
    nl, d, n = w_mod.shape
    r = c_all.shape[0]
    tn = _tile(n, 512)
    return pl.pallas_call(
        _mod_kernel,
        grid=(nl, n // tn),
        in_specs=[pl.BlockSpec((r, d), lambda l, j: (0, 0)),
                  pl.BlockSpec((None, d, tn), lambda l, j: (l, 0, j)),
                  pl.BlockSpec((None, 1, tn), lambda l, j: (l, 0, j))],
        out_specs=pl.BlockSpec((None, r, tn), lambda l, j: (l, 0, j)),
        out_shape=jax.ShapeDtypeStruct((nl, r, n), F32),
        compiler_params=_params("arbitrary", "arbitrary"),
        name="modulation",
    )(c_all, w_mod, b_mod.reshape(nl, 1, n))


def _norm_mod_rows(x, g, sc, sh):
    y = x * lax.rsqrt(jnp.mean(x * x, axis=-1, keepdims=True) + NORM_EPS) * g
    return (y * (1.0 + sc) + sh).astype(BF16)


def _norm_mod_kernel(x_ref, g_ref, sc_ref, sh_ref, o_ref):
    for s, rows in _stream_rows(sc_ref, x_ref.shape[0]):
        o_ref[rows, :] = _norm_mod_rows(x_ref[rows, :], g_ref[...], sc_ref[s], sh_ref[s])


def _norm_mod(x, g, sc, sh, t, tm):
    m, d = x.shape
    vec = _stream_vec_spec(t, tm, d, False)
    return pl.pallas_call(
        _norm_mod_kernel,
        grid=(m // tm,),
        in_specs=[pl.BlockSpec((tm, d), lambda i: (i, 0)),
                  pl.BlockSpec((1, d), lambda i: (0, 0)), vec, vec],
        out_specs=pl.BlockSpec((tm, d), lambda i: (i, 0)),
        out_shape=jax.ShapeDtypeStruct((m, d), BF16),
        compiler_params=_params("arbitrary"),
        name="norm_mod",
    )(x, g.reshape(1, d), sc[:, None, :], sh[:, None, :])


def _proj_kernel(*refs, rope):
    if rope:
        x_ref, w_ref, cos_ref, sin_ref, o_ref = refs
    else:
        x_ref, w_ref, o_ref = refs
    acc = _dot(x_ref[...], w_ref[...])
    if rope:
        cos, sin = cos_ref[...], sin_ref[...]
        for h in range(acc.shape[1] // D_HEAD):
            sl = slice(h * D_HEAD, (h + 1) * D_HEAD)
            xh = acc[:, sl]
            o_ref[:, sl] = (xh * cos + pltpu.roll(xh, D_HEAD // 2, 1) * sin).astype(o_ref.dtype)
    else:
        o_ref[...] = acc.astype(o_ref.dtype)


def _proj(x, w, col_off, n, out_dtype, tm, rope_tabs=None, t=None):
    m, k = x.shape
    tn = _tile(n, 1024)
    assert col_off % tn == 0
    joff = col_off // tn
    in_specs = [pl.BlockSpec((tm, k), lambda i, j: (i, 0)),
                pl.BlockSpec((k, tn), lambda i, j: (0, j + joff))]
    args = [x, w]
    if rope_tabs is not None:
        cos, sin = rope_tabs
        if t % tm == 0:
            nblk = t // tm
            tab_map = lambda i, j: (i % nblk, 0)
        else:
            assert tm % t == 0
            cos, sin = jnp.tile(cos, (tm // t, 1)), jnp.tile(sin, (tm // t, 1))
            tab_map = lambda i, j: (0, 0)
        in_specs += [pl.BlockSpec((tm, D_HEAD), tab_map), pl.BlockSpec((tm, D_HEAD), tab_map)]
        args += [cos, sin]
    return pl.pallas_call(
        functools.partial(_proj_kernel, rope=rope_tabs is not None),
        grid=(m // tm, n // tn),
        in_specs=in_specs,
        out_specs=pl.BlockSpec((tm, tn), lambda i, j: (i, j)),
        out_shape=jax.ShapeDtypeStruct((m, n), out_dtype),
        compiler_params=_params("arbitrary", "arbitrary"),
        name="proj_rope" if rope_tabs is not None else "proj",
    )(*args)


class _Masks:
    def __init__(self):
        self._cache = {}

    def _get(self, kind, q0, tq, k0, n):
        qp = np.arange(q0, q0 + tq)[:, None]
        kp = np.arange(k0, k0 + n)[None, :]
        vis = (kp // CHUNK <= qp // CHUNK) if kind == "chunk" else (kp < qp)
        if vis.all():
            return None
        assert vis.any()
        aligned = q0 % CHUNK == 0 and k0 % CHUNK == 0
        key = (kind, q0 - k0, tq, n) if (aligned or kind == "before") else (kind, q0, k0, tq, n)
        if key not in self._cache:
            row = lax.broadcasted_iota(jnp.int32, (tq, n), 0) + q0
            col = lax.broadcasted_iota(jnp.int32, (tq, n), 1) + k0
            self._cache[key] = (col // CHUNK <= row // CHUNK) if kind == "chunk" else (col < row)
        return self._cache[key]

    def chunk(self, q0, tq, k0, n):
        return self._get("chunk", q0, tq, k0, n)

    def before(self, q0, tq, k0, n):
        return self._get("before", q0, tq, k0, n)


def _lambda(lq1, lk1, lq2, lk2, lam_init):
    return (jnp.exp(jnp.sum(lq1[...] * lk1[...], keepdims=True))
            - jnp.exp(jnp.sum(lq2[...] * lk2[...], keepdims=True)) + lam_init)


def _diff_rows(q0, q1, segs, lam, subln_g, out_scale):
    scale = 1.0 / math.sqrt(D_HEAD)
    s0s, s1s = [], []
    for k0, k1, _, mask in segs:
        s0 = _dot_nt(q0, k0) * scale
        s1 = _dot_nt(q1, k1) * scale
        if mask is not None:
            s0 = jnp.where(mask, s0, NEG_INF)
            s1 = jnp.where(mask, s1, NEG_INF)
        s0s.append(s0)
        s1s.append(s1)
    m0 = functools.reduce(jnp.maximum, [jnp.max(s, axis=1, keepdims=True) for s in s0s])
    m1 = functools.reduce(jnp.maximum, [jnp.max(s, axis=1, keepdims=True) for s in s1s])
    p0s = [jnp.exp(s - m0) for s in s0s]
    p1s = [jnp.exp(s - m1) for s in s1s]
    l0 = functools.reduce(jnp.add, [jnp.sum(p, axis=1, keepdims=True) for p in p0s])
    l1 = functools.reduce(jnp.add, [jnp.sum(p, axis=1, keepdims=True) for p in p1s])
    inv0 = 1.0 / l0
    inv1 = lam / l1
    o = None
    for (_, _, v, _), p0, p1 in zip(segs, p0s, p1s):
        part = _dot((p0 * inv0 - p1 * inv1).astype(BF16), v)
        o = part if o is None else o + part
    o = o * lax.rsqrt(jnp.mean(o * o, axis=-1, keepdims=True) + SUBLN_EPS) * subln_g
    return o * out_scale


def _tri(n):
    j = lax.broadcasted_iota(jnp.int32, (n, n), 0)
    s = lax.broadcasted_iota(jnp.int32, (n, n), 1)
    return (j >= s).astype(BF16)


def _stick_rows(q, blocks, tris):
    tq = q.shape[0]
    scale = 1.0 / math.sqrt(D_HEAD)
    zs, sps = [], []
    for k, _, mask in blocks:
        z = _dot_nt(q, k) * scale
        sp = jnp.maximum(z, 0.0) + jnp.log(1.0 + jnp.exp(-jnp.abs(z)))
        if mask is not None:
            sp = jnp.where(mask, sp, 0.0)
        zs.append(z)
        sps.append(sp)
    incls = [None] * len(blocks)
    for n in sorted({sp.shape[1] for sp in sps}):
        idx = [i for i, sp in enumerate(sps) if sp.shape[1] == n]
        stacked = sps[idx[0]] if len(idx) == 1 else jnp.concatenate([sps[i] for i in idx], axis=0)
        hi = stacked.astype(BF16)
        lo = (stacked - hi.astype(F32)).astype(BF16)
        inc = _dot(hi, tris[n]) + _dot(lo, tris[n])
        for r, i in enumerate(idx):
            incls[i] = inc[r * tq:(r + 1) * tq, :]
    carry = None
    o = None
    for i in reversed(range(len(blocks))):
        _, v, mask = blocks[i]
        expo = zs[i] - incls[i]
        if carry is not None:
            expo = expo - carry
        a = jnp.exp(expo)
        if mask is not None:
            a = jnp.where(mask, a, 0.0)
        part = _dot(a.astype(BF16), v)
        o = part if o is None else o + part
        total = incls[i][:, 0:1]
        carry = total if carry is None else carry + total
    return o


def _attn_a_self_kernel(q_ref, k_ref, v_ref, lq1, lk1, lq2, lk2, g_ref, o_ref, kb, vb, *, tq, lam_init):
    s_len = q_ref.shape[0]
    kb[...] = k_ref[...].astype(BF16)
    vb[...] = v_ref[...].astype(BF16)
    lam = _lambda(lq1, lk1, lq2, lk2, lam_init)
    g = g_ref[...]
    masks = _Masks()
    for qi in range(s_len // tq):
        r0 = qi * tq
        rows = slice(r0, r0 + tq)
        segs = []
        if qi > 0:
            segs.append((kb[0:r0, :D_HEAD], kb[0:r0, D_HEAD:], vb[0:r0, :], masks.chunk(r0, tq, 0, r0)))
        segs.append((kb[rows, :D_HEAD], kb[rows, D_HEAD:], vb[rows, :], masks.chunk(r0, tq, r0, tq)))
        o = _diff_rows(q_ref[rows, :D_HEAD], q_ref[rows, D_HEAD:], segs, lam, g, 1.0 - lam_init)
        o_ref[rows, :] = o.astype(o_ref.dtype)


def _attn_a_self(q, k, v, lams, subln_g, b, s_len, lam_init):
    m, width = q.shape
    hw = 2 * D_HEAD
    tq = _tile(s_len, 256)
    assert tq % CHUNK == 0
    blk = pl.BlockSpec((s_len, hw), lambda i, h: (i, h))
    vec = pl.BlockSpec((1, D_HEAD), lambda i, h: (0, 0))
    return pl.pallas_call(
        functools.partial(_attn_a_self_kernel, tq=tq, lam_init=lam_init),
        grid=(b, width // hw),
        in_specs=[blk, blk, blk, vec, vec, vec, vec, pl.BlockSpec((1, hw), lambda i, h: (0, 0))],
        out_specs=blk,
        out_shape=jax.ShapeDtypeStruct((m, width), BF16),
        scratch_shapes=[pltpu.VMEM((s_len, hw), BF16), pltpu.VMEM((s_len, hw), BF16)],
        compiler_params=_params("arbitrary", "arbitrary"),
        name="diff_attn_self",
    )(q, k, v, *lams, subln_g)


def _attn_b_self_kernel(q_ref, k_ref, v_ref, o_ref, kb, vb, *, tq):
    s_len = q_ref.shape[0]
    kb[...] = k_ref[...].astype(BF16)
    vb[...] = v_ref[...].astype(BF16)
    tris = {tq: _tri(tq)}
    masks = _Masks()
    for qi in range(s_len // tq):
        r0 = qi * tq
        rows = slice(r0, r0 + tq)
        blocks = []
        for j in range(qi + 1):
            keys = slice(j * tq, (j + 1) * tq)
            blocks.append((kb[keys, :], vb[keys, :], masks.before(r0, tq, j * tq, tq)))
        o_ref[rows, :] = _stick_rows(q_ref[rows, :], blocks, tris).astype(o_ref.dtype)


def _attn_b_self(q, k, v, b, s_len):
    m, width = q.shape
    tq = _tile(s_len, 256)
    blk = pl.BlockSpec((s_len, D_HEAD), lambda i, h: (i, h))
    return pl.pallas_call(
        functools.partial(_attn_b_self_kernel, tq=tq),
        grid=(b, width // D_HEAD),
        in_specs=[blk, blk, blk],
        out_specs=blk,
        out_shape=jax.ShapeDtypeStruct((m, width), BF16),
        scratch_shapes=[pltpu.VMEM((s_len, D_HEAD), BF16), pltpu.VMEM((s_len, D_HEAD), BF16)],
        compiler_params=_params("arbitrary", "arbitrary"),
        name="stick_attn_self",
    )(q, k, v)


def _attn_a_cached_kernel(q_ref, kc_ref, vc_ref, kn_ref, vn_ref, lq1, lk1, lq2, lk2, g_ref, o_ref, *, lam_init):
    t = q_ref.shape[0]
    p = kc_ref.shape[0]
    kc = kc_ref[...].astype(BF16)
    vc = vc_ref[...].astype(BF16)
    kn = kn_ref[...].astype(BF16)
    vn = vn_ref[...].astype(BF16)
    lam = _lambda(lq1, lk1, lq2, lk2, lam_init)
    masks = _Masks()
    segs = [(kc[:, :D_HEAD], kc[:, D_HEAD:], vc, masks.chunk(p, t, 0, p)),
            (kn[:, :D_HEAD], kn[:, D_HEAD:], vn, masks.chunk(p, t, p, t))]
    o = _diff_rows(q_ref[:, :D_HEAD], q_ref[:, D_HEAD:], segs, lam, g_ref[...], 1.0 - lam_init)
    o_ref[...] = o.astype(o_ref.dtype)


def _attn_a_cached(q, kc, vc, kn, vn, lams, subln_g, b, t, lam_init):
    m, width = q.shape
    p = kc.shape[1]
    hw = 2 * D_HEAD
    blk = pl.BlockSpec((t, hw), lambda i, h: (i, h))
    cblk = pl.BlockSpec((None, p, hw), lambda i, h: (i, 0, h))
    vec = pl.BlockSpec((1, D_HEAD), lambda i, h: (0, 0))
    return pl.pallas_call(
        functools.partial(_attn_a_cached_kernel, lam_init=lam_init),
        grid=(b, width // hw),
        in_specs=[blk, cblk, cblk, blk, blk, vec, vec, vec, vec, pl.BlockSpec((1, hw), lambda i, h: (0, 0))],
        out_specs=blk,
        out_shape=jax.ShapeDtypeStruct((m, width), BF16),
        compiler_params=_params("arbitrary", "arbitrary"),
        name="diff_attn_cached",
    )(q, kc, vc, kn, vn, *lams, subln_g)


def _attn_b_cached_kernel(q_ref, kc_ref, vc_ref, kn_ref, vn_ref, o_ref, kb, vb, *, tk):
    t = q_ref.shape[0]
    p = kc_ref.shape[0]
    kb[...] = kc_ref[...].astype(BF16)
    vb[...] = vc_ref[...].astype(BF16)
    masks = _Masks()
    blocks = []
    for j in range(p // tk):
        keys = slice(j * tk, (j + 1) * tk)
        blocks.append((kb[keys, :], vb[keys, :], masks.before(p, t, j * tk, tk)))
    blocks.append((kn_ref[...].astype(BF16), vn_ref[...].astype(BF16), masks.before(p, t, p, t)))
    tris = {n: _tri(n) for n in {tk, t}}
    o_ref[...] = _stick_rows(q_ref[...], blocks, tris).astype(o_ref.dtype)


def _attn_b_cached(q, kc, vc, kn, vn, b, t):
    m, width = q.shape
    p = kc.shape[1]
    tk = _tile(p, 256)
    blk = pl.BlockSpec((t, D_HEAD), lambda i, h: (i, h))
    cblk = pl.BlockSpec((None, p, D_HEAD), lambda i, h: (i, 0, h))
    return pl.pallas_call(
        functools.partial(_attn_b_cached_kernel, tk=tk),
        grid=(b, width // D_HEAD),
        in_specs=[blk, cblk, cblk, blk, blk],
        out_specs=blk,
        out_shape=jax.ShapeDtypeStruct((m, width), BF16),
        scratch_shapes=[pltpu.VMEM((p, D_HEAD), BF16), pltpu.VMEM((p, D_HEAD), BF16)],
        compiler_params=_params("arbitrary", "arbitrary"),
        name="stick_attn_cached",
    )(q, kc, vc, kn, vn)


def _res_mm_kernel(*refs, nx):
    xs, ws = refs[:nx], refs[nx:2 * nx]
    xres_ref, gate_ref, o_ref = refs[2 * nx:]
    acc = _dot(xs[0][...], ws[0][...])
    for x_ref, w_ref in zip(xs[1:], ws[1:]):
        acc = acc + _dot(x_ref[...], w_ref[...])
    for s, rows in _stream_rows(gate_ref, o_ref.shape[0]):
        o_ref[rows, :] = xres_ref[rows, :] + gate_ref[s] * acc[rows, :]


def _res_matmul(xs, w, xres, gate, t, tm):
    m, n = xres.shape
    kp = xs[0].shape[1]
    nx = len(xs)
    assert all(x.shape == (m, kp) for x in xs) and w.shape == (nx * kp, n)
    tn = _tile(n, 512)
    in_specs = [pl.BlockSpec((tm, kp), lambda i, j: (i, 0)) for _ in xs]
    in_specs += [pl.BlockSpec((kp, tn), functools.partial(lambda i, j, p: (p, j), p=p)) for p in range(nx)]
    in_specs += [pl.BlockSpec((tm, tn), lambda i, j: (i, j)), _stream_vec_spec(t, tm, tn, True)]
    return pl.pallas_call(
        functools.partial(_res_mm_kernel, nx=nx),
        grid=(m // tm, n // tn),
        in_specs=in_specs,
        out_specs=pl.BlockSpec((tm, tn), lambda i, j: (i, j)),
        out_shape=jax.ShapeDtypeStruct((m, n), F32),
        compiler_params=_params("arbitrary", "arbitrary"),
        name="res_matmul",
    )(*xs, *([w] * nx), xres, gate[:, None, :])


def _mlp_kernel(*refs, final, nchunk):
    if final:
        x_ref, ng_ref, sc_ref, sh_ref, wu_ref, wd_ref, g_ref, fg_ref, o_ref, h_scr = refs
    else:
        x_ref, ng_ref, sc_ref, sh_ref, wu_ref, wd_ref, g_ref, o_ref, h_scr = refs
    f = pl.program_id(1)
    streams = _stream_rows(g_ref, o_ref.shape[0])

    @pl.when(f == 0)
    def _():
        for s, rows in streams:
            h_scr[rows, :] = _norm_mod_rows(x_ref[rows, :], ng_ref[...], sc_ref[s], sh_ref[s])

    hid = _dot(h_scr[...], wu_ref[...])
    hid = jnp.square(jnp.maximum(hid, 0.0)).astype(BF16)
    d = o_ref.shape[1]
    cols = [slice(c * (d // nchunk), (c + 1) * (d // nchunk)) for c in range(nchunk)]

    @pl.when(f == 0)
    def _():
        for sl in cols:
            o_ref[:, sl] = _dot(hid, wd_ref[:, sl])

    @pl.when(f > 0)
    def _():
        for sl in cols:
            o_ref[:, sl] += _dot(hid, wd_ref[:, sl])

    @pl.when(f == pl.num_programs(1) - 1)
    def _():
        for s, rows in streams:
            x = x_ref[rows, :] + g_ref[s] * o_ref[rows, :]
            if final:
                x = x * lax.rsqrt(jnp.mean(x * x, axis=-1, keepdims=True) + NORM_EPS) * fg_ref[...]
            o_ref[rows, :] = x


def _mlp(x, norm_g, sc, sh, wu, wd, layer, gate, t, tm, final_g=None):
    m, d = x.shape
    dff = wu.shape[2]
    tf = _tile(dff, 512)
    vec = _stream_vec_spec(t, tm, d, False)
    in_specs = [pl.BlockSpec((tm, d), lambda i, f: (i, 0), pipeline_mode=pl.Buffered(1)),
                pl.BlockSpec((1, d), lambda i, f: (0, 0)), vec, vec,
                pl.BlockSpec((None, d, tf), lambda i, f: (layer, 0, f)),
                pl.BlockSpec((None, tf, d), lambda i, f: (layer, f, 0)),
                vec]
    args = [x, norm_g.reshape(1, d), sc[:, None, :], sh[:, None, :], wu, wd, gate[:, None, :]]
    if final_g is not None:
        in_specs.append(pl.BlockSpec((1, d), lambda i, f: (0, 0)))
        args.append(final_g.reshape(1, d))
    return pl.pallas_call(
        functools.partial(_mlp_kernel, final=final_g is not None, nchunk=max(1, d // 512)),
        grid=(m // tm, dff // tf),
        in_specs=in_specs,
        out_specs=pl.BlockSpec((tm, d), lambda i, f: (i, 0)),
        out_shape=jax.ShapeDtypeStruct((m, d), F32),
        scratch_shapes=[pltpu.VMEM((tm, d), BF16)],
        compiler_params=_params("arbitrary", "arbitrary"),
        name="mlp_final" if final_g is not None else "mlp",
    )(*args)


def _glu_kernel(x_ref, wa_ref, wb_ref, o_ref):
    x = x_ref[...]
    o_ref[...] = _dot(x, wa_ref[...]) * jax.nn.sigmoid(_dot(x, wb_ref[...]))


def _glu(x, w, tm):
    m, k = x.shape
    d = w.shape[1] // 2
    tn = _tile(d, 512)
    nj = d // tn
    return pl.pallas_call(
        _glu_kernel,
        grid=(m // tm, nj),
        in_specs=[pl.BlockSpec((tm, k), lambda i, j: (i, 0)),
                  pl.BlockSpec((k, tn), lambda i, j: (0, j)),
                  pl.BlockSpec((k, tn), lambda i, j: (0, j + nj))],
        out_specs=pl.BlockSpec((tm, tn), lambda i, j: (i, j)),
        out_shape=jax.ShapeDtypeStruct((m, d), F32),
        compiler_params=_params("arbitrary", "arbitrary"),
        name="pointwise_glu",
    )(x, w, w)


def _conv_kernel(u_ref, prev_ref, st_ref, dw_ref, dwb_ref, lng_ref, lnb_ref, y_ref, ns_ref, up, shifted, yacc,
                 *, ts, lane_chunk, row_chunk):
    ti = pl.program_id(1)
    d = u_ref.shape[1]
    off = CONV_HALO - (CONV_WIDTH - 1)

    @pl.when(ti == 0)
    def _():
        up[0:CONV_HALO, :] = st_ref[...]

    @pl.when(ti > 0)
    def _():
        up[0:CONV_HALO, :] = prev_ref[...]

    up[CONV_HALO:CONV_HALO + ts, :] = u_ref[...]

    def lane_body(c, carry):
        cols = pl.ds(pl.multiple_of(c * lane_chunk, lane_chunk), lane_chunk)
        dw = dw_ref[:, cols]
        bias = dwb_ref[:, cols]
        nrow = CONV_HALO + ts - SUBLANE
        for b in range(1, SUBLANE):
            shifted[b - 1, 0:nrow, :] = up[b:b + nrow, cols]
        for r0 in range(0, ts, row_chunk):
            acc = None
            for w in range(CONV_WIDTH):
                a, b = divmod(off + w, SUBLANE)
                r = r0 + a * SUBLANE
                src = up[r:r + row_chunk, cols] if b == 0 else shifted[b - 1, r:r + row_chunk, :]
                term = src * dw[w:w + 1, :]
                acc = term if acc is None else acc + term
            yacc[r0:r0 + row_chunk, cols] = acc + bias
        return carry

    lax.fori_loop(0, d // lane_chunk, lane_body, 0)
    y = yacc[...]
    yc = y - jnp.mean(y, axis=-1, keepdims=True)
    var = jnp.mean(yc * yc, axis=-1, keepdims=True)
    yn = yc * lax.rsqrt(var + NORM_EPS) * lng_ref[...] + lnb_ref[...]
    y_ref[...] = (yn * jax.nn.sigmoid(yn)).astype(y_ref.dtype)

    @pl.when(ti == pl.num_programs(1) - 1)
    def _():
        ns_ref[...] = up[CONV_HALO + ts - (CONV_WIDTH - 1):CONV_HALO + ts, :]


def _conv_ln_swish(u, state, dw, dw_b, ln_g, ln_b, b, s_len):
    m, d = u.shape
    ts = _tile(s_len, 256)
    assert ts % CONV_HALO == 0 and s_len >= CONV_WIDTH - 1
    nt = s_len // ts
    st = jnp.pad(state, ((0, 0), (CONV_HALO - (CONV_WIDTH - 1), 0), (0, 0)))
    hpb = ts // CONV_HALO
    vec = pl.BlockSpec((1, d), lambda i, t: (0, 0))
    lane_chunk = _tile(d, 256)
    y, ns = pl.pallas_call(
        functools.partial(_conv_kernel, ts=ts, lane_chunk=lane_chunk, row_chunk=_tile(ts, 128)),
        grid=(b, nt),
        in_specs=[pl.BlockSpec((ts, d), lambda i, t: (i * nt + t, 0)),
                  pl.BlockSpec((CONV_HALO, d), lambda i, t: (jnp.maximum((i * nt + t) * hpb - 1, 0), 0)),
                  pl.BlockSpec((None, CONV_HALO, d), lambda i, t: (i, 0, 0)),
                  pl.BlockSpec((CONV_WIDTH, d), lambda i, t: (0, 0)),
                  vec, vec, vec],
        out_specs=[pl.BlockSpec((ts, d), lambda i, t: (i * nt + t, 0)),
                   pl.BlockSpec((None, CONV_WIDTH - 1, d), lambda i, t: (i, 0, 0))],
        out_shape=[jax.ShapeDtypeStruct((m, d), BF16),
                   jax.ShapeDtypeStruct((b, CONV_WIDTH - 1, d), F32)],
        scratch_shapes=[pltpu.VMEM((CONV_HALO + ts, d), F32),
                        pltpu.VMEM((SUBLANE - 1, CONV_HALO + ts, lane_chunk), F32),
                        pltpu.VMEM((ts, d), F32)],
        compiler_params=_params("arbitrary", "arbitrary"),
        name="conv_ln_swish",
    )(u, u, st, dw, dw_b.reshape(1, d), ln_g.reshape(1, d), ln_b.reshape(1, d))
    return y, ns


def _rope_tables(pos):
    half = D_HEAD // 2
    inv_freq = jnp.power(ROPE_THETA, -jnp.arange(half, dtype=F32) / half)
    ang = pos.astype(F32)[:, None] * inv_freq[None, :]
    cos, sin = jnp.cos(ang), jnp.sin(ang)
    return jnp.concatenate([cos, cos], axis=-1), jnp.concatenate([-sin, sin], axis=-1)


def _trunk(x3, mod, cache, state, p, wb):
    b, t, d = x3.shape
    m = b * t
    mix_a = d // 2
    mix_b = d - mix_a
    x = x3.reshape(m, d)
    tm = _tile(m, 1024)
    tm_mlp = _tile(m, 512)
    tm_norm = _tile(m, 512) if t % _tile(m, 512) else _tile(t, 256)
    past_len = 0 if cache is None else cache[0].shape[1]
    rope_tabs = _rope_tables(past_len + jnp.arange(t))

    def split_mod(layer):
        return jnp.split(mod[layer], 6, axis=-1)

    sh_m, sc_m, g_m, sh_f, sc_f, g_f = split_mod(0)
    h = _norm_mod(x, p["norm_mix"][0], sc_m, sh_m, t, tm_norm)
    w_in = wb["w_attn_in"]
    qa = _proj(h, w_in, 0, mix_a, BF16, tm, rope_tabs, t)
    ka = _proj(h, w_in, mix_a, mix_a, F32, tm, rope_tabs, t)
    va = _proj(h, w_in, 2 * mix_a, mix_a, F32, tm)
    qb = _proj(h, w_in, 3 * mix_a, mix_b, BF16, tm)
    kb = _proj(h, w_in, 3 * mix_a + mix_b, mix_b, F32, tm)
    vb = _proj(h, w_in, 3 * mix_a + 2 * mix_b, mix_b, F32, tm)
    lams = [p[n][0].reshape(1, D_HEAD) for n in ("lambda_q1", "lambda_k1", "lambda_q2", "lambda_k2")]
    subln_g = p["diff_subln_g"][0].reshape(1, 2 * D_HEAD)
    lam_init = 0.8 - 0.6 * math.exp(-0.3 * 0)
    if cache is None:
        o_a = _attn_a_self(qa, ka, va, lams, subln_g, b, t, lam_init)
        o_b = _attn_b_self(qb, kb, vb, b, t)
    else:
        ck_a, cv_a, ck_b, cv_b = cache
        o_a = _attn_a_cached(qa, ck_a.reshape(b, past_len, mix_a), cv_a.reshape(b, past_len, mix_a),
                             ka, va, lams, subln_g, b, t, lam_init)
        o_b = _attn_b_cached(qb, ck_b.reshape(b, past_len, mix_b), cv_b.reshape(b, past_len, mix_b),
                             kb, vb, b, t)
    x = _res_matmul([o_a, o_b], wb["w_attn_out"], x, g_m, t, tm)
    x = _mlp(x, p["norm_mlp"][0], sc_f, sh_f, wb["mlp_up"], wb["mlp_down"], 0, g_f, t, tm_mlp)

    sh_m, sc_m, g_m, sh_f, sc_f, g_f = split_mod(1)
    h = _norm_mod(x, p["norm_mix"][1], sc_m, sh_m, t, tm_norm)
    u = _glu(h, wb["conv_pw1"], tm)
    if state is None:
        state = jnp.zeros((b, CONV_WIDTH - 1, d), F32)
    y, new_state = _conv_ln_swish(u, state, p["conv_dw"][0], p["conv_dw_b"][0], p["conv_ln_g"][0],
                                  p["conv_ln_b"][0], b, t)
    x = _res_matmul([y], wb["conv_pw2"], x, g_m, t, tm)
    y_out = _mlp(x, p["norm_mlp"][1], sc_f, sh_f, wb["mlp_up"], wb["mlp_down"], 1, g_f, t, tm_mlp,
                 final_g=p["final_g"])

    h_a, h_b = mix_a // (2 * D_HEAD), mix_b // D_HEAD
    return (y_out.reshape(b, t, d),
            ka.reshape(1, b, t, 2 * h_a, D_HEAD), va.reshape(1, b, t, h_a, 2 * D_HEAD),
            kb.reshape(1, b, t, h_b, D_HEAD), vb.reshape(1, b, t, h_b, D_HEAD),
            new_state[None])


def kernel(x_prompt, x_sample, c_prompt, c_sample, cache_k_diff, cache_v_diff, cache_k_sb, cache_v_sb, state_conv, w_mod, b_mod, norm_mix, norm_mlp, w_attn_in, w_attn_out, lambda_q1, lambda_k1, lambda_q2, lambda_k2, diff_subln_g, conv_pw1, conv_dw, conv_dw_b, conv_ln_g, conv_ln_b, conv_pw2, mlp_up, mlp_down, final_g):
    assert w_mod.shape[0] == 2 and w_attn_in.shape[0] == 1 and conv_pw1.shape[0] == 1
    p = dict(norm_mix=norm_mix, norm_mlp=norm_mlp, lambda_q1=lambda_q1, lambda_k1=lambda_k1,
             lambda_q2=lambda_q2, lambda_k2=lambda_k2, diff_subln_g=diff_subln_g, conv_dw=conv_dw,
             conv_dw_b=conv_dw_b, conv_ln_g=conv_ln_g, conv_ln_b=conv_ln_b, final_g=final_g)
    wb = dict(w_attn_in=w_attn_in[0].astype(BF16), w_attn_out=w_attn_out[0].astype(BF16),
              conv_pw1=conv_pw1[0].astype(BF16), conv_pw2=conv_pw2[0].astype(BF16),
              mlp_up=mlp_up.astype(BF16), mlp_down=mlp_down.astype(BF16))
    nb = c_prompt.shape[0]
    mod = _modulation(jnp.concatenate([c_prompt, c_sample], axis=0), w_mod, b_mod)
    out_p = _trunk(x_prompt, mod[:, :nb], None, None, p, wb)
    cache = (cache_k_diff[0], cache_v_diff[0], cache_k_sb[0], cache_v_sb[0])
    out_s = _trunk(x_sample, mod[:, nb:], cache, state_conv[0], p, wb)
    return (out_p[0], out_s[0]) + out_p[1:] + out_s[1:]
```

```python
import functools
import math

import jax
import jax.numpy as jnp
import numpy as np
from jax import lax
from jax.experimental import pallas as pl
from jax.experimental.pallas import tpu as pltpu

CHUNK = 64
D_HEAD = 128
CONV_WIDTH = 31
ROPE_THETA = 10000.0
NORM_EPS = 1e-6
SUBLN_EPS = 1e-5
NEG_INF = -1e30

F32 = jnp.float32
BF16 = jnp.bfloat16

V7X_VMEM_LIMIT_BYTES = 60000 * 1024
SUBLANE = 8
CONV_HALO = -(-(CONV_WIDTH - 1) // SUBLANE) * SUBLANE


def _params(*sem):
    return pltpu.CompilerParams(dimension_semantics=sem, vmem_limit_bytes=V7X_VMEM_LIMIT_BYTES)


def _tile(n, pref):
    t = min(n, pref)
    while n % t:
        t -= 1
    return t


def _dot(a, b):
    return jnp.dot(a, b, preferred_element_type=F32)


def _dot_nt(a, b):
    return lax.dot_general(a, b, (((1,), (1,)), ((), ())), preferred_element_type=F32)


def _stream_vec_spec(t, tm, width, with_col):
    if t % tm == 0:
        spt, tps = 1, t // tm
    else:
        assert tm % t == 0
        spt, tps = tm // t, 1
    if with_col:
        return pl.BlockSpec((spt, 1, width), lambda i, j: (i // tps, 0, j))
    return pl.BlockSpec((spt, 1, width), lambda i, *_: (i // tps, 0, 0))


def _stream_rows(vec_ref, tm):
    spt = vec_ref.shape[0]
    rows = tm // spt
    return [(s, slice(s * rows, (s + 1) * rows)) for s in range(spt)]


def _mod_kernel(c_ref, w_ref, b_ref, o_ref):
    c = c_ref[...]
    cs = (c * jax.nn.sigmoid(c)).astype(BF16)
    o_ref[...] = _dot(cs, w_ref[...].astype(BF16)) + b_ref[...]


def _modulation(c_all, w_mod, b_mod):
    nl, d, n = w_mod.shape
    r = c_all.shape[0]
    tn = _tile(n, 512)
    return pl.pallas_call(
        _mod_kernel,
        grid=(nl, n // tn),
        in_specs=[pl.BlockSpec((r, d), lambda l, j: (0, 0)),
                  pl.BlockSpec((None, d, tn), lambda l, j: (l, 0, j)),
                  pl.BlockSpec((None, 1, tn), lambda l, j: (l, 0, j))],
        out_specs=pl.BlockSpec((None, r, tn), lambda l, j: (l, 0, j)),
        out_shape=jax.ShapeDtypeStruct((nl, r, n), F32),
        compiler_params=_params("arbitrary", "arbitrary"),
        name="modulation",
    )(c_all, w_mod, b_mod.reshape(nl, 1, n))


def _norm_mod_rows(x, g, sc, sh):
    y = x * lax.rsqrt(jnp.mean(x * x, axis=-1, keepdims=True) + NORM_EPS) * g
    return (y * (1.0 + sc) + sh).astype(BF16)


def _norm_mod_kernel(x_ref, g_ref, sc_ref, sh_ref, o_ref):
    for s, rows in _stream_rows(sc_ref, x_ref.shape[0]):
        o_ref[rows, :] = _norm_mod_rows(x_ref[rows, :], g_ref[...], sc_ref[s], sh_ref[s])


def _norm_mod(x, g, sc, sh, t, tm):
    m, d = x.shape
    vec = _stream_vec_spec(t, tm, d, False)
    return pl.pallas_call(
        _norm_mod_kernel,
        grid=(m // tm,),
        in_specs=[pl.BlockSpec((tm, d), lambda i: (i, 0)),
                  pl.BlockSpec((1, d), lambda i: (0, 0)), vec, vec],
        out_specs=pl.BlockSpec((tm, d), lambda i: (i, 0)),
        out_shape=jax.ShapeDtypeStruct((m, d), BF16),
        compiler_params=_params("arbitrary"),
        name="norm_mod",
    )(x, g.reshape(1, d), sc[:, None, :], sh[:, None, :])


def _proj_kernel(*refs, rope):
    if rope:
        x_ref, w_ref, cos_ref, sin_ref, o_ref = refs
    else:
        x_ref, w_ref, o_ref = refs
    acc = _dot(x_ref[...], w_ref[...])
    if rope:
        cos, sin = cos_ref[...], sin_ref[...]
        for h in range(acc.shape[1] // D_HEAD):
            sl = slice(h * D_HEAD, (h + 1) * D_HEAD)
            xh = acc[:, sl]
            o_ref[:, sl] = (xh * cos + pltpu.roll(xh, D_HEAD // 2, 1) * sin).astype(o_ref.dtype)
    else:
        o_ref[...] = acc.astype(o_ref.dtype)


def _proj(x, w, col_off, n, out_dtype, tm, rope_tabs=None, t=None, tn_pref=1024):
    m, k = x.shape
    tn = _tile(n, tn_pref)
    assert col_off % tn == 0
    joff = col_off // tn
    in_specs = [pl.BlockSpec((tm, k), lambda i, j: (i, 0)),
                pl.BlockSpec((k, tn), lambda i, j: (0, j + joff))]
    args = [x, w]
    if rope_tabs is not None:
        cos, sin = rope_tabs
        if t % tm == 0:
            nblk = t // tm
            tab_map = lambda i, j: (i % nblk, 0)
        else:
            assert tm % t == 0
            cos, sin = jnp.tile(cos, (tm // t, 1)), jnp.tile(sin, (tm // t, 1))
            tab_map = lambda i, j: (0, 0)
        in_specs += [pl.BlockSpec((tm, D_HEAD), tab_map), pl.BlockSpec((tm, D_HEAD), tab_map)]
        args += [cos, sin]
    return pl.pallas_call(
        functools.partial(_proj_kernel, rope=rope_tabs is not None),
        grid=(m // tm, n // tn),
        in_specs=in_specs,
        out_specs=pl.BlockSpec((tm, tn), lambda i, j: (i, j)),
        out_shape=jax.ShapeDtypeStruct((m, n), out_dtype),
        compiler_params=_params("arbitrary", "arbitrary"),
        name="proj_rope" if rope_tabs is not None else "proj",
    )(*args)


class _Masks:
    def __init__(self):
        self._cache = {}

    def _get(self, kind, q0, tq, k0, n):
        qp = np.arange(q0, q0 + tq)[:, None]
        kp = np.arange(k0, k0 + n)[None, :]
        vis = (kp // CHUNK <= qp // CHUNK) if kind == "chunk" else (kp < qp)
        if vis.all():
            return None
        assert vis.any()
        aligned = q0 % CHUNK == 0 and k0 % CHUNK == 0
        key = (kind, q0 - k0, tq, n) if (aligned or kind == "before") else (kind, q0, k0, tq, n)
        if key not in self._cache:
            row = lax.broadcasted_iota(jnp.int32, (tq, n), 0) + q0
            col = lax.broadcasted_iota(jnp.int32, (tq, n), 1) + k0
            self._cache[key] = (col // CHUNK <= row // CHUNK) if kind == "chunk" else (col < row)
        return self._cache[key]

    def chunk(self, q0, tq, k0, n):
        return self._get("chunk", q0, tq, k0, n)

    def before(self, q0, tq, k0, n):
        return self._get("before", q0, tq, k0, n)


def _lambda(lq1, lk1, lq2, lk2, lam_init):
    return (jnp.exp(jnp.sum(lq1[...] * lk1[...], keepdims=True))
            - jnp.exp(jnp.sum(lq2[...] * lk2[...], keepdims=True)) + lam_init)


def _diff_scores(q0, q1, segs):
    scale = 1.0 / math.sqrt(D_HEAD)
    s0s, s1s = [], []
    for k0, k1, _, mask in segs:
        s0 = _dot_nt(q0, k0) * scale
        s1 = _dot_nt(q1, k1) * scale
        if mask is not None:
            s0 = jnp.where(mask, s0, NEG_INF)
            s1 = jnp.where(mask, s1, NEG_INF)
        s0s.append(s0)
        s1s.append(s1)
    return s0s, s1s


def _diff_output(s0s, s1s, segs, lam, subln_g, out_scale):
    m0 = functools.reduce(jnp.maximum, [jnp.max(s, axis=1, keepdims=True) for s in s0s])
    m1 = functools.reduce(jnp.maximum, [jnp.max(s, axis=1, keepdims=True) for s in s1s])
    p0s = [jnp.exp(s - m0) for s in s0s]
    p1s = [jnp.exp(s - m1) for s in s1s]
    l0 = functools.reduce(jnp.add, [jnp.sum(p, axis=1, keepdims=True) for p in p0s])
    l1 = functools.reduce(jnp.add, [jnp.sum(p, axis=1, keepdims=True) for p in p1s])
    inv0 = 1.0 / l0
    inv1 = lam / l1
    o = None
    for (_, _, v, _), p0, p1 in zip(segs, p0s, p1s):
        part = _dot((p0 * inv0 - p1 * inv1).astype(BF16), v)
        o = part if o is None else o + part
    o = o * lax.rsqrt(jnp.mean(o * o, axis=-1, keepdims=True) + SUBLN_EPS) * subln_g
    return o * out_scale


def _tri(n):
    j = lax.broadcasted_iota(jnp.int32, (2 * n, n), 0)
    s = lax.broadcasted_iota(jnp.int32, (2 * n, n), 1)
    return (jnp.where(j >= n, j - n, j) >= s).astype(BF16)


def _stick_scores(q, blocks):
    scale = 1.0 / math.sqrt(D_HEAD)
    zs, splits = [], []
    for k, _, mask in blocks:
        z = _dot_nt(q, k) * scale
        sp = jnp.maximum(z, 0.0) + jnp.log(1.0 + jnp.exp(-jnp.abs(z)))
        if mask is not None:
            sp = jnp.where(mask, sp, 0.0)
        hi = sp.astype(BF16)
        lo = (sp - hi.astype(F32)).astype(BF16)
        zs.append(z)
        splits.append(jnp.concatenate([hi, lo], axis=1))
    return zs, splits


def _stick_output(zs, splits, blocks, tris):
    tq = zs[0].shape[0]
    incls = [None] * len(blocks)
    for n in sorted({z.shape[1] for z in zs}):
        idx = [i for i, z in enumerate(zs) if z.shape[1] == n]
        stacked = splits[idx[0]] if len(idx) == 1 else jnp.concatenate([splits[i] for i in idx], axis=0)
        inc = _dot(stacked, tris[n])
        for r, i in enumerate(idx):
            incls[i] = inc[r * tq:(r + 1) * tq, :]
    carry = None
    o = None
    for i in reversed(range(len(blocks))):
        _, v, mask = blocks[i]
        expo = zs[i] - incls[i]
        if carry is not None:
            expo = expo - carry
        a = jnp.exp(expo)
        if mask is not None:
            a = jnp.where(mask, a, 0.0)
        part = _dot(a.astype(BF16), v)
        o = part if o is None else o + part
        total = incls[i][:, 0:1]
        carry = total if carry is None else carry + total
    return o


def _attn_a_self_kernel(q_ref, k_ref, v_ref, lq1, lk1, lq2, lk2, g_ref, o_ref, kb, vb, *, tq, lam_init):
    s_len = q_ref.shape[0]
    kb[...] = k_ref[...].astype(BF16)
    vb[...] = v_ref[...].astype(BF16)
    lam = _lambda(lq1, lk1, lq2, lk2, lam_init)
    g = g_ref[...]
    masks = _Masks()
    nq = s_len // tq
    pending = None
    for qi in range(nq + 1):
        stage1 = None
        if qi < nq:
            r0 = qi * tq
            rows = slice(r0, r0 + tq)
            segs = []
            if qi > 0:
                segs.append((kb[0:r0, :D_HEAD], kb[0:r0, D_HEAD:], vb[0:r0, :], masks.chunk(r0, tq, 0, r0)))
            segs.append((kb[rows, :D_HEAD], kb[rows, D_HEAD:], vb[rows, :], masks.chunk(r0, tq, r0, tq)))
            stage1 = (rows, segs) + _diff_scores(q_ref[rows, :D_HEAD], q_ref[rows, D_HEAD:], segs)
        if pending is not None:
            prows, psegs, s0s, s1s = pending
            o_ref[prows, :] = _diff_output(s0s, s1s, psegs, lam, g, 1.0 - lam_init).astype(o_ref.dtype)
        pending = stage1


def _attn_a_self(q, k, v, lams, subln_g, b, s_len, lam_init):
    m, width = q.shape
    hw = 2 * D_HEAD
    tq = _tile(s_len, 256)
    assert tq % CHUNK == 0
    blk = pl.BlockSpec((s_len, hw), lambda i, h: (i, h))
    vec = pl.BlockSpec((1, D_HEAD), lambda i, h: (0, 0))
    return pl.pallas_call(
        functools.partial(_attn_a_self_kernel, tq=tq, lam_init=lam_init),
        grid=(b, width // hw),
        in_specs=[blk, blk, blk, vec, vec, vec, vec, pl.BlockSpec((1, hw), lambda i, h: (0, 0))],
        out_specs=blk,
        out_shape=jax.ShapeDtypeStruct((m, width), BF16),
        scratch_shapes=[pltpu.VMEM((s_len, hw), BF16), pltpu.VMEM((s_len, hw), BF16)],
        compiler_params=_params("arbitrary", "arbitrary"),
        name="diff_attn_self",
    )(q, k, v, *lams, subln_g)


def _attn_b_self_kernel(q_ref, k_ref, v_ref, o_ref, kb, vb, *, tq):
    s_len = q_ref.shape[0]
    kb[...] = k_ref[...].astype(BF16)
    vb[...] = v_ref[...].astype(BF16)
    tris = {tq: _tri(tq)}
    masks = _Masks()
    pending = None
    nq = s_len // tq
    for step in range(nq + 1):
        if step < nq:
            r0 = step * tq
            rows = slice(r0, r0 + tq)
            blocks = []
            for j in range(step + 1):
                keys = slice(j * tq, (j + 1) * tq)
                blocks.append((kb[keys, :], vb[keys, :], masks.before(r0, tq, j * tq, tq)))
            stage1 = (rows, blocks) + _stick_scores(q_ref[rows, :], blocks)
        else:
            stage1 = None
        if pending is not None:
            prows, pblocks, zs, splits = pending
            o_ref[prows, :] = _stick_output(zs, splits, pblocks, tris).astype(o_ref.dtype)
        pending = stage1


def _attn_b_self(q, k, v, b, s_len):
    m, width = q.shape
    tq = _tile(s_len, 256)
    blk = pl.BlockSpec((s_len, D_HEAD), lambda i, h: (i, h))
    return pl.pallas_call(
        functools.partial(_attn_b_self_kernel, tq=tq),
        grid=(b, width // D_HEAD),
        in_specs=[blk, blk, blk],
        out_specs=blk,
        out_shape=jax.ShapeDtypeStruct((m, width), BF16),
        scratch_shapes=[pltpu.VMEM((s_len, D_HEAD), BF16), pltpu.VMEM((s_len, D_HEAD), BF16)],
        compiler_params=_params("arbitrary", "arbitrary"),
        name="stick_attn_self",
    )(q, k, v)


def _attn_a_cached_kernel(q_ref, kc_ref, vc_ref, kn_ref, vn_ref, lq1, lk1, lq2, lk2, g_ref, o_ref, *, lam_init):
    t = q_ref.shape[0]
    p = kc_ref.shape[0]
    kc = kc_ref[...].astype(BF16)
    vc = vc_ref[...].astype(BF16)
    kn = kn_ref[...].astype(BF16)
    vn = vn_ref[...].astype(BF16)
    lam = _lambda(lq1, lk1, lq2, lk2, lam_init)
    masks = _Masks()
    segs = [(kc[:, :D_HEAD], kc[:, D_HEAD:], vc, masks.chunk(p, t, 0, p)),
            (kn[:, :D_HEAD], kn[:, D_HEAD:], vn, masks.chunk(p, t, p, t))]
    s0s, s1s = _diff_scores(q_ref[:, :D_HEAD], q_ref[:, D_HEAD:], segs)
    o_ref[...] = _diff_output(s0s, s1s, segs, lam, g_ref[...], 1.0 - lam_init).astype(o_ref.dtype)


def _attn_a_cached(q, kc, vc, kn, vn, lams, subln_g, b, t, lam_init):
    m, width = q.shape
    p = kc.shape[1]
    hw = 2 * D_HEAD
    blk = pl.BlockSpec((t, hw), lambda i, h: (i, h))
    cblk = pl.BlockSpec((None, p, hw), lambda i, h: (i, 0, h))
    vec = pl.BlockSpec((1, D_HEAD), lambda i, h: (0, 0))
    return pl.pallas_call(
        functools.partial(_attn_a_cached_kernel, lam_init=lam_init),
        grid=(b, width // hw),
        in_specs=[blk, cblk, cblk, blk, blk, vec, vec, vec, vec, pl.BlockSpec((1, hw), lambda i, h: (0, 0))],
        out_specs=blk,
        out_shape=jax.ShapeDtypeStruct((m, width), BF16),
        compiler_params=_params("arbitrary", "arbitrary"),
        name="diff_attn_cached",
    )(q, kc, vc, kn, vn, *lams, subln_g)


def _attn_b_cached_kernel(q_ref, kc_ref, vc_ref, kn_ref, vn_ref, o_ref, *, tk):
    t = q_ref.shape[0]
    p = kc_ref.shape[0]
    kc = kc_ref[...].astype(BF16)
    vc = vc_ref[...].astype(BF16)
    masks = _Masks()
    blocks = []
    for j in range(p // tk):
        keys = slice(j * tk, (j + 1) * tk)
        blocks.append((kc[keys, :], vc[keys, :], masks.before(p, t, j * tk, tk)))
    blocks.append((kn_ref[...].astype(BF16), vn_ref[...].astype(BF16), masks.before(p, t, p, t)))
    tris = {n: _tri(n) for n in {tk, t}}
    zs, splits = _stick_scores(q_ref[...], blocks)
    o_ref[...] = _stick_output(zs, splits, blocks, tris).astype(o_ref.dtype)


def _attn_b_cached(q, kc, vc, kn, vn, b, t):
    m, width = q.shape
    p = kc.shape[1]
    tk = _tile(p, 256)
    blk = pl.BlockSpec((t, D_HEAD), lambda i, h: (i, h))
    cblk = pl.BlockSpec((None, p, D_HEAD), lambda i, h: (i, 0, h))
    return pl.pallas_call(
        functools.partial(_attn_b_cached_kernel, tk=tk),
        grid=(b, width // D_HEAD),
        in_specs=[blk, cblk, cblk, blk, blk],
        out_specs=blk,
        out_shape=jax.ShapeDtypeStruct((m, width), BF16),
        compiler_params=_params("arbitrary", "arbitrary"),
        name="stick_attn_cached",
    )(q, kc, vc, kn, vn)


def _res_mm_kernel(*refs, nx):
    xs, ws = refs[:nx], refs[nx:2 * nx]
    xres_ref, gate_ref, o_ref = refs[2 * nx:]
    acc = _dot(xs[0][...], ws[0][...])
    for x_ref, w_ref in zip(xs[1:], ws[1:]):
        acc = acc + _dot(x_ref[...], w_ref[...])
    for s, rows in _stream_rows(gate_ref, o_ref.shape[0]):
        o_ref[rows, :] = xres_ref[rows, :] + gate_ref[s] * acc[rows, :]


def _res_matmul(xs, w, xres, gate, t, tm, tn_pref=512):
    m, n = xres.shape
    kp = xs[0].shape[1]
    nx = len(xs)
    assert all(x.shape == (m, kp) for x in xs) and w.shape == (nx * kp, n)
    tn = _tile(n, tn_pref)
    in_specs = [pl.BlockSpec((tm, kp), lambda i, j: (i, 0)) for _ in xs]
    in_specs += [pl.BlockSpec((kp, tn), functools.partial(lambda i, j, p: (p, j), p=p)) for p in range(nx)]
    in_specs += [pl.BlockSpec((tm, tn), lambda i, j: (i, j)), _stream_vec_spec(t, tm, tn, True)]
    return pl.pallas_call(
        functools.partial(_res_mm_kernel, nx=nx),
        grid=(m // tm, n // tn),
        in_specs=in_specs,
        out_specs=pl.BlockSpec((tm, tn), lambda i, j: (i, j)),
        out_shape=jax.ShapeDtypeStruct((m, n), F32),
        compiler_params=_params("arbitrary", "arbitrary"),
        name="res_matmul",
    )(*xs, *([w] * nx), xres, gate[:, None, :])


def _mlp_kernel(*refs, final, nchunk):
    if final:
        x_ref, ng_ref, sc_ref, sh_ref, wu_ref, wd_ref, g_ref, fg_ref, o_ref, h_scr = refs
    else:
        x_ref, ng_ref, sc_ref, sh_ref, wu_ref, wd_ref, g_ref, o_ref, h_scr = refs
    f = pl.program_id(1)
    streams = _stream_rows(g_ref, o_ref.shape[0])

    @pl.when(f == 0)
    def _():
        for s, rows in streams:
            h_scr[rows, :] = _norm_mod_rows(x_ref[rows, :], ng_ref[...], sc_ref[s], sh_ref[s])

    hid = _dot(h_scr[...], wu_ref[...])
    hid = jnp.square(jnp.maximum(hid, 0.0)).astype(BF16)
    d = o_ref.shape[1]
    cols = [slice(c * (d // nchunk), (c + 1) * (d // nchunk)) for c in range(nchunk)]

    @pl.when(f == 0)
    def _():
        for sl in cols:
            o_ref[:, sl] = _dot(hid, wd_ref[:, sl])

    @pl.when(f > 0)
    def _():
        for sl in cols:
            o_ref[:, sl] += _dot(hid, wd_ref[:, sl])

    @pl.when(f == pl.num_programs(1) - 1)
    def _():
        for s, rows in streams:
            x = x_ref[rows, :] + g_ref[s] * o_ref[rows, :]
            if final:
                x = x * lax.rsqrt(jnp.mean(x * x, axis=-1, keepdims=True) + NORM_EPS) * fg_ref[...]
            o_ref[rows, :] = x


def _mlp(x, norm_g, sc, sh, wu, wd, layer, gate, t, tm, final_g=None, tf_pref=512, out_mode=None):
    m, d = x.shape
    dff = wu.shape[2]
    tf = _tile(dff, tf_pref)
    vec = _stream_vec_spec(t, tm, d, False)
    in_specs = [pl.BlockSpec((tm, d), lambda i, f: (i, 0), pipeline_mode=pl.Buffered(1)),
                pl.BlockSpec((1, d), lambda i, f: (0, 0)), vec, vec,
                pl.BlockSpec((None, d, tf), lambda i, f: (layer, 0, f)),
                pl.BlockSpec((None, tf, d), lambda i, f: (layer, f, 0)),
                vec]
    args = [x, norm_g.reshape(1, d), sc[:, None, :], sh[:, None, :], wu, wd, gate[:, None, :]]
    if final_g is not None:
        in_specs.append(pl.BlockSpec((1, d), lambda i, f: (0, 0)))
        args.append(final_g.reshape(1, d))
    return pl.pallas_call(
        functools.partial(_mlp_kernel, final=final_g is not None, nchunk=max(1, d // 512)),
        grid=(m // tm, dff // tf),
        in_specs=in_specs,
        out_specs=pl.BlockSpec((tm, d), lambda i, f: (i, 0), pipeline_mode=out_mode),
        out_shape=jax.ShapeDtypeStruct((m, d), F32),
        scratch_shapes=[pltpu.VMEM((tm, d), BF16)],
        compiler_params=_params("arbitrary", "arbitrary"),
        name="mlp_final" if final_g is not None else "mlp",
    )(*args)


def _glu_kernel(x_ref, wa_ref, wb_ref, o_ref):
    x = x_ref[...]
    o_ref[...] = _dot(x, wa_ref[...]) * jax.nn.sigmoid(_dot(x, wb_ref[...]))


def _glu(x, w, tm):
    m, k = x.shape
    d = w.shape[1] // 2
    tn = _tile(d, 512)
    nj = d // tn
    return pl.pallas_call(
        _glu_kernel,
        grid=(m // tm, nj),
        in_specs=[pl.BlockSpec((tm, k), lambda i, j: (i, 0)),
                  pl.BlockSpec((k, tn), lambda i, j: (0, j)),
                  pl.BlockSpec((k, tn), lambda i, j: (0, j + nj))],
        out_specs=pl.BlockSpec((tm, tn), lambda i, j: (i, j)),
        out_shape=jax.ShapeDtypeStruct((m, d), F32),
        compiler_params=_params("arbitrary", "arbitrary"),
        name="pointwise_glu",
    )(x, w, w)


def _conv_kernel(u_ref, prev_ref, st_ref, dw_ref, dwb_ref, lng_ref, lnb_ref, y_ref, ns_ref, up, shifted, yacc,
                 *, ts, lane_chunk, row_chunk):
    ti = pl.program_id(1)
    d = u_ref.shape[1]
    off = CONV_HALO - (CONV_WIDTH - 1)

    @pl.when(ti == 0)
    def _():
        up[0:CONV_HALO, :] = st_ref[...]

    @pl.when(ti > 0)
    def _():
        up[0:CONV_HALO, :] = prev_ref[...]

    up[CONV_HALO:CONV_HALO + ts, :] = u_ref[...]

    def lane_body(c, carry):
        cols = pl.ds(pl.multiple_of(c * lane_chunk, lane_chunk), lane_chunk)
        dw = dw_ref[:, cols]
        bias = dwb_ref[:, cols]
        nrow = CONV_HALO + ts - SUBLANE
        for b in range(1, SUBLANE):
            shifted[b - 1, 0:nrow, :] = up[b:b + nrow, cols]
        for r0 in range(0, ts, row_chunk):
            acc = None
            for w in range(CONV_WIDTH):
                a, b = divmod(off + w, SUBLANE)
                r = r0 + a * SUBLANE
                src = up[r:r + row_chunk, cols] if b == 0 else shifted[b - 1, r:r + row_chunk, :]
                term = src * dw[w:w + 1, :]
                acc = term if acc is None else acc + term
            yacc[r0:r0 + row_chunk, cols] = acc + bias
        return carry

    lax.fori_loop(0, d // lane_chunk, lane_body, 0)
    y = yacc[...]
    yc = y - jnp.mean(y, axis=-1, keepdims=True)
    var = jnp.mean(yc * yc, axis=-1, keepdims=True)
    yn = yc * lax.rsqrt(var + NORM_EPS) * lng_ref[...] + lnb_ref[...]
    y_ref[...] = (yn * jax.nn.sigmoid(yn)).astype(y_ref.dtype)

    @pl.when(ti == pl.num_programs(1) - 1)
    def _():
        ns_ref[...] = up[CONV_HALO + ts - (CONV_WIDTH - 1):CONV_HALO + ts, :]


def _conv_ln_swish(u, state, dw, dw_b, ln_g, ln_b, b, s_len):
    m, d = u.shape
    ts = _tile(s_len, 256)
    assert ts % CONV_HALO == 0 and s_len >= CONV_WIDTH - 1
    nt = s_len // ts
    st = jnp.pad(state, ((0, 0), (CONV_HALO - (CONV_WIDTH - 1), 0), (0, 0)))
    hpb = ts // CONV_HALO
    vec = pl.BlockSpec((1, d), lambda i, t: (0, 0))
    lane_chunk = _tile(d, 256)
    y, ns = pl.pallas_call(
        functools.partial(_conv_kernel, ts=ts, lane_chunk=lane_chunk, row_chunk=_tile(ts, 128)),
        grid=(b, nt),
        in_specs=[pl.BlockSpec((ts, d), lambda i, t: (i * nt + t, 0)),
                  pl.BlockSpec((CONV_HALO, d), lambda i, t: (jnp.maximum((i * nt + t) * hpb - 1, 0), 0)),
                  pl.BlockSpec((None, CONV_HALO, d), lambda i, t: (i, 0, 0)),
                  pl.BlockSpec((CONV_WIDTH, d), lambda i, t: (0, 0)),
                  vec, vec, vec],
        out_specs=[pl.BlockSpec((ts, d), lambda i, t: (i * nt + t, 0)),
                   pl.BlockSpec((None, CONV_WIDTH - 1, d), lambda i, t: (i, 0, 0))],
        out_shape=[jax.ShapeDtypeStruct((m, d), BF16),
                   jax.ShapeDtypeStruct((b, CONV_WIDTH - 1, d), F32)],
        scratch_shapes=[pltpu.VMEM((CONV_HALO + ts, d), F32),
                        pltpu.VMEM((SUBLANE - 1, CONV_HALO + ts, lane_chunk), F32),
                        pltpu.VMEM((ts, d), F32)],
        compiler_params=_params("arbitrary", "arbitrary"),
        name="conv_ln_swish",
    )(u, u, st, dw, dw_b.reshape(1, d), ln_g.reshape(1, d), ln_b.reshape(1, d))
    return y, ns


def _rope_tables(pos):
    half = D_HEAD // 2
    inv_freq = jnp.power(ROPE_THETA, -jnp.arange(half, dtype=F32) / half)
    ang = pos.astype(F32)[:, None] * inv_freq[None, :]
    cos, sin = jnp.cos(ang), jnp.sin(ang)
    return jnp.concatenate([cos, cos], axis=-1), jnp.concatenate([-sin, sin], axis=-1)


def _trunk(x3, mod, cache, state, p, wb):
    b, t, d = x3.shape
    m = b * t
    mix_a = d // 2
    mix_b = d - mix_a
    x = x3.reshape(m, d)
    tm = _tile(m, 1024)
    tm_mlp = _tile(m, 512)
    tm_norm = _tile(m, 512) if t % _tile(m, 512) else _tile(t, 256)
    past_len = 0 if cache is None else cache[0].shape[1]
    rope_tabs = _rope_tables(past_len + jnp.arange(t))

    def split_mod(layer):
        return jnp.split(mod[layer], 6, axis=-1)

    sh_m, sc_m, g_m, sh_f, sc_f, g_f = split_mod(0)
    h = _norm_mod(x, p["norm_mix"][0], sc_m, sh_m, t, tm_norm)
    w_in = wb["w_attn_in"]
    qa = _proj(h, w_in, 0, mix_a, BF16, tm, rope_tabs, t)
    ka = _proj(h, w_in, mix_a, mix_a, F32, tm, rope_tabs, t)
    va = _proj(h, w_in, 2 * mix_a, mix_a, F32, tm, tn_pref=512)
    qb = _proj(h, w_in, 3 * mix_a, mix_b, BF16, tm)
    kb = _proj(h, w_in, 3 * mix_a + mix_b, mix_b, F32, tm)
    vb = _proj(h, w_in, 3 * mix_a + 2 * mix_b, mix_b, F32, tm)
    lams = [p[n][0].reshape(1, D_HEAD) for n in ("lambda_q1", "lambda_k1", "lambda_q2", "lambda_k2")]
    subln_g = p["diff_subln_g"][0].reshape(1, 2 * D_HEAD)
    lam_init = 0.8 - 0.6 * math.exp(-0.3 * 0)
    if cache is None:
        o_a = _attn_a_self(qa, ka, va, lams, subln_g, b, t, lam_init)
        o_b = _attn_b_self(qb, kb, vb, b, t)
    else:
        ck_a, cv_a, ck_b, cv_b = cache
        o_a = _attn_a_cached(qa, ck_a.reshape(b, past_len, mix_a), cv_a.reshape(b, past_len, mix_a),
                             ka, va, lams, subln_g, b, t, lam_init)
        o_b = _attn_b_cached(qb, ck_b.reshape(b, past_len, mix_b), cv_b.reshape(b, past_len, mix_b),
                             kb, vb, b, t)
    x = _res_matmul([o_a, o_b], wb["w_attn_out"], x, g_m, t, tm)
    x = _mlp(x, p["norm_mlp"][0], sc_f, sh_f, wb["mlp_up"], wb["mlp_down"], 0, g_f, t, tm_mlp)

    sh_m, sc_m, g_m, sh_f, sc_f, g_f = split_mod(1)
    h = _norm_mod(x, p["norm_mix"][1], sc_m, sh_m, t, tm_norm)
    u = _glu(h, wb["conv_pw1"], tm)
    if state is None:
        state = jnp.zeros((b, CONV_WIDTH - 1, d), F32)
    y, new_state = _conv_ln_swish(u, state, p["conv_dw"][0], p["conv_dw_b"][0], p["conv_ln_g"][0],
                                  p["conv_ln_b"][0], b, t)
    x = _res_matmul([y], wb["conv_pw2"], x, g_m, t, tm)
    y_out = _mlp(x, p["norm_mlp"][1], sc_f, sh_f, wb["mlp_up"], wb["mlp_down"], 1, g_f, t, tm_mlp,
                 final_g=p["final_g"])

    h_a, h_b = mix_a // (2 * D_HEAD), mix_b // D_HEAD
    return (y_out.reshape(b, t, d),
            ka.reshape(1, b, t, 2 * h_a, D_HEAD), va.reshape(1, b, t, h_a, 2 * D_HEAD),
            kb.reshape(1, b, t, h_b, D_HEAD), vb.reshape(1, b, t, h_b, D_HEAD),
            new_state[None])


def kernel(x_prompt, x_sample, c_prompt, c_sample, cache_k_diff, cache_v_diff, cache_k_sb, cache_v_sb, state_conv, w_mod, b_mod, norm_mix, norm_mlp, w_attn_in, w_attn_out, lambda_q1, lambda_k1, lambda_q2, lambda_k2, diff_subln_g, conv_pw1, conv_dw, conv_dw_b, conv_ln_g, conv_ln_b, conv_pw2, mlp_up, mlp_down, final_g):
    assert w_mod.shape[0] == 2 and w_attn_in.shape[0] == 1 and conv_pw1.shape[0] == 1
    p = dict(norm_mix=norm_mix, norm_mlp=norm_mlp, lambda_q1=lambda_q1, lambda_k1=lambda_k1,
             lambda_q2=lambda_q2, lambda_k2=lambda_k2, diff_subln_g=diff_subln_g, conv_dw=conv_dw,
             conv_dw_b=conv_dw_b, conv_ln_g=conv_ln_g, conv_ln_b=conv_ln_b, final_g=final_g)
    wb = dict(w_attn_in=w_attn_in[0].astype(BF16), w_attn_out=w_attn_out[0].astype(BF16),
              conv_pw1=conv_pw1[0].astype(BF16), conv_pw2=conv_pw2[0].astype(BF16),
              mlp_up=mlp_up.astype(BF16), mlp_down=mlp_down.astype(BF16))
    nb = c_prompt.shape[0]
    mod = _modulation(jnp.concatenate([c_prompt, c_sample], axis=0), w_mod, b_mod)
    out_p = _trunk(x_prompt, mod[:, :nb], None, None, p, wb)
    cache = (cache_k_diff[0], cache_v_diff[0], cache_k_sb[0], cache_v_sb[0])
    out_s = _trunk(x_sample, mod[:, nb:], cache, state_conv[0], p, wb)
    return (out_p[0], out_s[0]) + out_p[1:] + out_s[1:]
```

```python
import functools
import math

import jax
import jax.numpy as jnp
import numpy as np
from jax import lax
from jax.experimental import pallas as pl
from jax.experimental.pallas import tpu as pltpu

CHUNK = 64
D_HEAD = 128
CONV_WIDTH = 31
ROPE_THETA = 10000.0
NORM_EPS = 1e-6
SUBLN_EPS = 1e-5
NEG_INF = -1e30

F32 = jnp.float32
BF16 = jnp.bfloat16

V7X_VMEM_LIMIT_BYTES = 60000 * 1024
SUBLANE = 8
MLP_FF_TILE = 512
CONV_HALO = -(-(CONV_WIDTH - 1) // SUBLANE) * SUBLANE


def _params(*sem):
    return pltpu.CompilerParams(dimension_semantics=sem, vmem_limit_bytes=V7X_VMEM_LIMIT_BYTES)


def _tile(n, pref):
    t = min(n, pref)
    while n % t:
        t -= 1
    return t


def _dot(a, b):
    return jnp.dot(a, b, preferred_element_type=F32)


def _dot_nt(a, b):
    return lax.dot_general(a, b, (((1,), (1,)), ((), ())), preferred_element_type=F32)


def _stream_vec_spec(t, tm, width, with_col):
    if t % tm == 0:
        spt, tps = 1, t // tm
    else:
        assert tm % t == 0
        spt, tps = tm // t, 1
    if with_col:
        return pl.BlockSpec((spt, 1, width), lambda i, j: (i // tps, 0, j))
    return pl.BlockSpec((spt, 1, width), lambda i, *_: (i // tps, 0, 0))


def _stream_rows(vec_ref, tm):
    spt = vec_ref.shape[0]
    rows = tm // spt
    return [(s, slice(s * rows, (s + 1) * rows)) for s in range(spt)]


def _mod_kernel(c_ref, w_ref, b_ref, o_ref):
    c = c_ref[...]
    cs = (c * jax.nn.sigmoid(c)).astype(BF16)
    o_ref[...] = _dot(cs, w_ref[...].astype(BF16)) + b_ref[...]


def _modulation(c_all, w_mod, b_mod):
    nl, d, n = w_mod.shape
    r = c_all.shape[0]
    tn = _tile(n, 512)
    return pl.pallas_call(
        _mod_kernel,
        grid=(nl, n // tn),
        in_specs=[pl.BlockSpec((r, d), lambda l, j: (0, 0)),
                  pl.BlockSpec((None, d, tn), lambda l, j: (l, 0, j)),
                  pl.BlockSpec((None, 1, tn), lambda l, j: (l, 0, j))],
        out_specs=pl.BlockSpec((None, r, tn), lambda l, j: (l, 0, j)),
        out_shape=jax.ShapeDtypeStruct((nl, r, n), F32),
        compiler_params=_params("arbitrary", "arbitrary"),
        name="modulation",
    )(c_all, w_mod, b_mod.reshape(nl, 1, n))


def _norm_mod_rows(x, g, sc, sh):
    y = x * lax.rsqrt(jnp.mean(x * x, axis=-1, keepdims=True) + NORM_EPS) * g
    return (y * (1.0 + sc) + sh).astype(BF16)


def _norm_mod_kernel(x_ref, g_ref, sc_ref, sh_ref, o_ref):
    for s, rows in _stream_rows(sc_ref, x_ref.shape[0]):
        o_ref[rows, :] = _norm_mod_rows(x_ref[rows, :], g_ref[...], sc_ref[s], sh_ref[s])


def _norm_mod(x, g, sc, sh, t, tm):
    m, d = x.shape
    vec = _stream_vec_spec(t, tm, d, False)
    return pl.pallas_call(
        _norm_mod_kernel,
        grid=(m // tm,),
        in_specs=[pl.BlockSpec((tm, d), lambda i: (i, 0)),
                  pl.BlockSpec((1, d), lambda i: (0, 0)), vec, vec],
        out_specs=pl.BlockSpec((tm, d), lambda i: (i, 0)),
        out_shape=jax.ShapeDtypeStruct((m, d), BF16),
        compiler_params=_params("arbitrary"),
        name="norm_mod",
    )(x, g.reshape(1, d), sc[:, None, :], sh[:, None, :])


def _proj_kernel(*refs, rope):
    if rope:
        x_ref, w_ref, cos_ref, sin_ref, o_ref = refs
    else:
        x_ref, w_ref, o_ref = refs
    acc = _dot(x_ref[...], w_ref[...])
    if rope:
        cos, sin = cos_ref[...], sin_ref[...]
        for h in range(acc.shape[1] // D_HEAD):
            sl = slice(h * D_HEAD, (h + 1) * D_HEAD)
            xh = acc[:, sl]
            o_ref[:, sl] = (xh * cos + pltpu.roll(xh, D_HEAD // 2, 1) * sin).astype(o_ref.dtype)
    else:
        o_ref[...] = acc.astype(o_ref.dtype)


def _proj(x, w, col_off, n, out_dtype, tm, rope_tabs=None, t=None, w_resident=False):
    m, k = x.shape
    tn = _tile(n, 1024)
    assert col_off % tn == 0
    joff = col_off // tn

    def spec(shape, fn):
        return pl.BlockSpec(shape, (lambda j, i: fn(i, j)) if w_resident else fn)

    in_specs = [spec((tm, k), lambda i, j: (i, 0)), spec((k, tn), lambda i, j: (0, j + joff))]
    args = [x, w]
    if rope_tabs is not None:
        cos, sin = rope_tabs
        if t % tm == 0:
            nblk = t // tm
            tab_map = lambda i, j: (i % nblk, 0)
        else:
            assert tm % t == 0
            cos, sin = jnp.tile(cos, (tm // t, 1)), jnp.tile(sin, (tm // t, 1))
            tab_map = lambda i, j: (0, 0)
        in_specs += [spec((tm, D_HEAD), tab_map), spec((tm, D_HEAD), tab_map)]
        args += [cos, sin]
    return pl.pallas_call(
        functools.partial(_proj_kernel, rope=rope_tabs is not None),
        grid=(n // tn, m // tm) if w_resident else (m // tm, n // tn),
        in_specs=in_specs,
        out_specs=spec((tm, tn), lambda i, j: (i, j)),
        out_shape=jax.ShapeDtypeStruct((m, n), out_dtype),
        compiler_params=_params("arbitrary", "arbitrary"),
        name="proj_rope" if rope_tabs is not None else "proj",
    )(*args)


class _Masks:
    def __init__(self):
        self._cache = {}

    def _get(self, kind, q0, tq, k0, n):
        qp = np.arange(q0, q0 + tq)[:, None]
        kp = np.arange(k0, k0 + n)[None, :]
        vis = (kp // CHUNK <= qp // CHUNK) if kind == "chunk" else (kp < qp)
        if vis.all():
            return None
        assert vis.any()
        aligned = q0 % CHUNK == 0 and k0 % CHUNK == 0
        key = (kind, q0 - k0, tq, n) if (aligned or kind == "before") else (kind, q0, k0, tq, n)
        if key not in self._cache:
            row = lax.broadcasted_iota(jnp.int32, (tq, n), 0) + q0
            col = lax.broadcasted_iota(jnp.int32, (tq, n), 1) + k0
            self._cache[key] = (col // CHUNK <= row // CHUNK) if kind == "chunk" else (col < row)
        return self._cache[key]

    def chunk(self, q0, tq, k0, n):
        return self._get("chunk", q0, tq, k0, n)

    def before(self, q0, tq, k0, n):
        return self._get("before", q0, tq, k0, n)


def _lambda(lq1, lk1, lq2, lk2, lam_init):
    return (jnp.exp(jnp.sum(lq1[...] * lk1[...], keepdims=True))
            - jnp.exp(jnp.sum(lq2[...] * lk2[...], keepdims=True)) + lam_init)


def _diff_scores(q0, q1, segs):
    scale = 1.0 / math.sqrt(D_HEAD)
    s0s, s1s = [], []
    for k0, k1, _, mask in segs:
        s0 = _dot_nt(q0, k0) * scale
        s1 = _dot_nt(q1, k1) * scale
        if mask is not None:
            s0 = jnp.where(mask, s0, NEG_INF)
            s1 = jnp.where(mask, s1, NEG_INF)
        s0s.append(s0)
        s1s.append(s1)
    return s0s, s1s


def _diff_output(s0s, s1s, segs, lam, subln_g, out_scale):
    m0 = functools.reduce(jnp.maximum, [jnp.max(s, axis=1, keepdims=True) for s in s0s])
    m1 = functools.reduce(jnp.maximum, [jnp.max(s, axis=1, keepdims=True) for s in s1s])
    p0s = [jnp.exp(s - m0) for s in s0s]
    p1s = [jnp.exp(s - m1) for s in s1s]
    l0 = functools.reduce(jnp.add, [jnp.sum(p, axis=1, keepdims=True) for p in p0s])
    l1 = functools.reduce(jnp.add, [jnp.sum(p, axis=1, keepdims=True) for p in p1s])
    inv0 = 1.0 / l0
    inv1 = lam / l1
    o = None
    for (_, _, v, _), p0, p1 in zip(segs, p0s, p1s):
        part = _dot((p0 * inv0 - p1 * inv1).astype(BF16), v)
        o = part if o is None else o + part
    o = o * lax.rsqrt(jnp.mean(o * o, axis=-1, keepdims=True) + SUBLN_EPS) * subln_g
    return o * out_scale


def _tri(n):
    j = lax.broadcasted_iota(jnp.int32, (2 * n, n), 0)
    s = lax.broadcasted_iota(jnp.int32, (2 * n, n), 1)
    return (jnp.where(j >= n, j - n, j) >= s).astype(BF16)


def _stick_scores(q, blocks):
    scale = 1.0 / math.sqrt(D_HEAD)
    zs, splits = [], []
    for k, _, mask in blocks:
        z = _dot_nt(q, k) * scale
        sp = jnp.maximum(z, 0.0) + jnp.log(1.0 + jnp.exp(-jnp.abs(z)))
        if mask is not None:
            sp = jnp.where(mask, sp, 0.0)
        hi = sp.astype(BF16)
        lo = (sp - hi.astype(F32)).astype(BF16)
        zs.append(z)
        splits.append(jnp.concatenate([hi, lo], axis=1))
    return zs, splits


def _stick_output(zs, splits, blocks, tris):
    tq = zs[0].shape[0]
    incls = [None] * len(blocks)
    for n in sorted({z.shape[1] for z in zs}):
        idx = [i for i, z in enumerate(zs) if z.shape[1] == n]
        stacked = splits[idx[0]] if len(idx) == 1 else jnp.concatenate([splits[i] for i in idx], axis=0)
        inc = _dot(stacked, tris[n])
        for r, i in enumerate(idx):
            incls[i] = inc[r * tq:(r + 1) * tq, :]
    carry = None
    o = None
    for i in reversed(range(len(blocks))):
        _, v, mask = blocks[i]
        expo = zs[i] - incls[i]
        if carry is not None:
            expo = expo - carry
        a = jnp.exp(expo)
        if mask is not None:
            a = jnp.where(mask, a, 0.0)
        part = _dot(a.astype(BF16), v)
        o = part if o is None else o + part
        total = incls[i][:, 0:1]
        carry = total if carry is None else carry + total
    return o


def _attn_a_self_kernel(q_ref, k_ref, v_ref, lq1, lk1, lq2, lk2, g_ref, o_ref, kb, vb, *, tq, lam_init):
    s_len = q_ref.shape[0]
    kb[...] = k_ref[...].astype(BF16)
    vb[...] = v_ref[...].astype(BF16)
    lam = _lambda(lq1, lk1, lq2, lk2, lam_init)
    g = g_ref[...]
    masks = _Masks()
    nq = s_len // tq
    pending = None
    for qi in range(nq + 1):
        stage1 = None
        if qi < nq:
            r0 = qi * tq
            rows = slice(r0, r0 + tq)
            segs = []
            if qi > 0:
                segs.append((kb[0:r0, :D_HEAD], kb[0:r0, D_HEAD:], vb[0:r0, :], masks.chunk(r0, tq, 0, r0)))
            segs.append((kb[rows, :D_HEAD], kb[rows, D_HEAD:], vb[rows, :], masks.chunk(r0, tq, r0, tq)))
            stage1 = (rows, segs) + _diff_scores(q_ref[rows, :D_HEAD], q_ref[rows, D_HEAD:], segs)
        if pending is not None:
            prows, psegs, s0s, s1s = pending
            o_ref[prows, :] = _diff_output(s0s, s1s, psegs, lam, g, 1.0 - lam_init).astype(o_ref.dtype)
        pending = stage1


def _attn_a_self(q, k, v, lams, subln_g, b, s_len, lam_init):
    m, width = q.shape
    hw = 2 * D_HEAD
    tq = _tile(s_len, 256)
    assert tq % CHUNK == 0
    blk = pl.BlockSpec((s_len, hw), lambda i, h: (i, h))
    vec = pl.BlockSpec((1, D_HEAD), lambda i, h: (0, 0))
    return pl.pallas_call(
        functools.partial(_attn_a_self_kernel, tq=tq, lam_init=lam_init),
        grid=(b, width // hw),
        in_specs=[blk, blk, blk, vec, vec, vec, vec, pl.BlockSpec((1, hw), lambda i, h: (0, 0))],
        out_specs=blk,
        out_shape=jax.ShapeDtypeStruct((m, width), BF16),
        scratch_shapes=[pltpu.VMEM((s_len, hw), BF16), pltpu.VMEM((s_len, hw), BF16)],
        compiler_params=_params("arbitrary", "arbitrary"),
        name="diff_attn_self",
    )(q, k, v, *lams, subln_g)


def _attn_b_self_kernel(q_ref, k_ref, v_ref, o_ref, kb, vb, *, tq):
    s_len = q_ref.shape[0]
    kb[...] = k_ref[...].astype(BF16)
    vb[...] = v_ref[...].astype(BF16)
    tris = {tq: _tri(tq)}
    masks = _Masks()
    pending = None
    nq = s_len // tq
    work = [(hh, qi) for hh in range(q_ref.shape[1] // D_HEAD) for qi in range(nq)]
    for item in work + [None]:
        stage1 = None
        if item is not None:
            hh, qi = item
            cols = slice(hh * D_HEAD, (hh + 1) * D_HEAD)
            r0 = qi * tq
            rows = slice(r0, r0 + tq)
            blocks = []
            for j in range(qi + 1):
                keys = slice(j * tq, (j + 1) * tq)
                blocks.append((kb[keys, cols], vb[keys, cols], masks.before(r0, tq, j * tq, tq)))
            stage1 = (rows, cols, blocks) + _stick_scores(q_ref[rows, cols], blocks)
        if pending is not None:
            prows, pcols, pblocks, zs, splits = pending
            o_ref[prows, pcols] = _stick_output(zs, splits, pblocks, tris).astype(o_ref.dtype)
        pending = stage1


def _attn_b_self(q, k, v, b, s_len):
    m, width = q.shape
    tq = _tile(s_len, 256)
    hw = _tile(width, 2 * D_HEAD)
    blk = pl.BlockSpec((s_len, hw), lambda i, h: (i, h))
    return pl.pallas_call(
        functools.partial(_attn_b_self_kernel, tq=tq),
        grid=(b, width // hw),
        in_specs=[blk, blk, blk],
        out_specs=blk,
        out_shape=jax.ShapeDtypeStruct((m, width), BF16),
        scratch_shapes=[pltpu.VMEM((s_len, hw), BF16), pltpu.VMEM((s_len, hw), BF16)],
        compiler_params=_params("arbitrary", "arbitrary"),
        name="stick_attn_self",
    )(q, k, v)


def _attn_a_cached_kernel(q_ref, kc_ref, vc_ref, kn_ref, vn_ref, lq1, lk1, lq2, lk2, g_ref, o_ref, *, lam_init):
    t = q_ref.shape[0]
    p = kc_ref.shape[0]
    kc = kc_ref[...].astype(BF16)
    vc = vc_ref[...].astype(BF16)
    kn = kn_ref[...].astype(BF16)
    vn = vn_ref[...].astype(BF16)
    lam = _lambda(lq1, lk1, lq2, lk2, lam_init)
    masks = _Masks()
    segs = [(kc[:, :D_HEAD], kc[:, D_HEAD:], vc, masks.chunk(p, t, 0, p)),
            (kn[:, :D_HEAD], kn[:, D_HEAD:], vn, masks.chunk(p, t, p, t))]
    s0s, s1s = _diff_scores(q_ref[:, :D_HEAD], q_ref[:, D_HEAD:], segs)
    o_ref[...] = _diff_output(s0s, s1s, segs, lam, g_ref[...], 1.0 - lam_init).astype(o_ref.dtype)


def _attn_a_cached(q, kc, vc, kn, vn, lams, subln_g, b, t, lam_init):
    m, width = q.shape
    p = kc.shape[1]
    hw = 2 * D_HEAD
    blk = pl.BlockSpec((t, hw), lambda i, h: (i, h))
    cblk = pl.BlockSpec((None, p, hw), lambda i, h: (i, 0, h))
    vec = pl.BlockSpec((1, D_HEAD), lambda i, h: (0, 0))
    return pl.pallas_call(
        functools.partial(_attn_a_cached_kernel, lam_init=lam_init),
        grid=(b, width // hw),
        in_specs=[blk, cblk, cblk, blk, blk, vec, vec, vec, vec, pl.BlockSpec((1, hw), lambda i, h: (0, 0))],
        out_specs=blk,
        out_shape=jax.ShapeDtypeStruct((m, width), BF16),
        compiler_params=_params("arbitrary", "arbitrary"),
        name="diff_attn_cached",
    )(q, kc, vc, kn, vn, *lams, subln_g)


def _attn_b_cached_kernel(q_ref, kc_ref, vc_ref, kn_ref, vn_ref, o_ref, *, tk):
    t = q_ref.shape[0]
    p = kc_ref.shape[0]
    kc = kc_ref[...].astype(BF16)
    vc = vc_ref[...].astype(BF16)
    masks = _Masks()
    blocks = []
    for j in range(p // tk):
        keys = slice(j * tk, (j + 1) * tk)
        blocks.append((kc[keys, :], vc[keys, :], masks.before(p, t, j * tk, tk)))
    blocks.append((kn_ref[...].astype(BF16), vn_ref[...].astype(BF16), masks.before(p, t, p, t)))
    tris = {n: _tri(n) for n in {tk, t}}
    zs, splits = _stick_scores(q_ref[...], blocks)
    o_ref[...] = _stick_output(zs, splits, blocks, tris).astype(o_ref.dtype)


def _attn_b_cached(q, kc, vc, kn, vn, b, t):
    m, width = q.shape
    p = kc.shape[1]
    tk = _tile(p, 256)
    blk = pl.BlockSpec((t, D_HEAD), lambda i, h: (i, h))
    cblk = pl.BlockSpec((None, p, D_HEAD), lambda i, h: (i, 0, h))
    return pl.pallas_call(
        functools.partial(_attn_b_cached_kernel, tk=tk),
        grid=(b, width // D_HEAD),
        in_specs=[blk, cblk, cblk, blk, blk],
        out_specs=blk,
        out_shape=jax.ShapeDtypeStruct((m, width), BF16),
        compiler_params=_params("arbitrary", "arbitrary"),
        name="stick_attn_cached",
    )(q, kc, vc, kn, vn)


def _res_mm_kernel(*refs, nx):
    xs, ws = refs[:nx], refs[nx:2 * nx]
    xres_ref, gate_ref, o_ref = refs[2 * nx:]
    acc = _dot(xs[0][...], ws[0][...])
    for x_ref, w_ref in zip(xs[1:], ws[1:]):
        acc = acc + _dot(x_ref[...], w_ref[...])
    for s, rows in _stream_rows(gate_ref, o_ref.shape[0]):
        o_ref[rows, :] = xres_ref[rows, :] + gate_ref[s] * acc[rows, :]


def _res_matmul(xs, w, xres, gate, t, tm, tn_pref=512):
    m, n = xres.shape
    kp = xs[0].shape[1]
    nx = len(xs)
    assert all(x.shape == (m, kp) for x in xs) and w.shape == (nx * kp, n)
    tn = _tile(n, tn_pref)
    in_specs = [pl.BlockSpec((tm, kp), lambda i, j: (i, 0)) for _ in xs]
    in_specs += [pl.BlockSpec((kp, tn), functools.partial(lambda i, j, p: (p, j), p=p)) for p in range(nx)]
    in_specs += [pl.BlockSpec((tm, tn), lambda i, j: (i, j)), _stream_vec_spec(t, tm, tn, True)]
    return pl.pallas_call(
        functools.partial(_res_mm_kernel, nx=nx),
        grid=(m // tm, n // tn),
        in_specs=in_specs,
        out_specs=pl.BlockSpec((tm, tn), lambda i, j: (i, j)),
        out_shape=jax.ShapeDtypeStruct((m, n), F32),
        compiler_params=_params("arbitrary", "arbitrary"),
        name="res_matmul",
    )(*xs, *([w] * nx), xres, gate[:, None, :])


def _mlp_kernel(*refs, final, nchunk):
    if final:
        x_ref, ng_ref, sc_ref, sh_ref, wu_ref, wd_ref, g_ref, fg_ref, o_ref, h_scr = refs
    else:
        x_ref, ng_ref, sc_ref, sh_ref, wu_ref, wd_ref, g_ref, o_ref, h_scr = refs
    f = pl.program_id(1)
    streams = _stream_rows(g_ref, o_ref.shape[0])

    @pl.when(f == 0)
    def _():
        for s, rows in streams:
            h_scr[rows, :] = _norm_mod_rows(x_ref[rows, :], ng_ref[...], sc_ref[s], sh_ref[s])

    hid = _dot(h_scr[...], wu_ref[...])
    hid = jnp.square(jnp.maximum(hid, 0.0)).astype(BF16)
    d = o_ref.shape[1]
    cols = [slice(c * (d // nchunk), (c + 1) * (d // nchunk)) for c in range(nchunk)]

    @pl.when(f == 0)
    def _():
        for sl in cols:
            o_ref[:, sl] = _dot(hid, wd_ref[:, sl])

    @pl.when(f > 0)
    def _():
        for sl in cols:
            o_ref[:, sl] += _dot(hid, wd_ref[:, sl])

    @pl.when(f == pl.num_programs(1) - 1)
    def _():
        for s, rows in streams:
            x = x_ref[rows, :] + g_ref[s] * o_ref[rows, :]
            if final:
                x = x * lax.rsqrt(jnp.mean(x * x, axis=-1, keepdims=True) + NORM_EPS) * fg_ref[...]
            o_ref[rows, :] = x


def _mlp_up_tiles(w_up):
    d, dff = w_up.shape
    tf = _tile(dff, MLP_FF_TILE)
    return w_up.astype(BF16).reshape(d, dff // tf, tf).transpose(1, 0, 2)


def _mlp(x, norm_g, sc, sh, wu, wd, layer, gate, t, tm, final_g=None):
    m, d = x.shape
    dff = wd.shape[1]
    tf = _tile(dff, MLP_FF_TILE)
    vec = _stream_vec_spec(t, tm, d, False)
    if wu.shape == (dff // tf, d, tf):
        wu_spec = pl.BlockSpec((None, d, tf), lambda i, f: (f, 0, 0))
    else:
        wu_spec = pl.BlockSpec((None, d, tf), lambda i, f: (layer, 0, f))
    in_specs = [pl.BlockSpec((tm, d), lambda i, f: (i, 0), pipeline_mode=pl.Buffered(1)),
                pl.BlockSpec((1, d), lambda i, f: (0, 0)), vec, vec,
                wu_spec,
                pl.BlockSpec((None, tf, d), lambda i, f: (layer, f, 0)),
                vec]
    args = [x, norm_g.reshape(1, d), sc[:, None, :], sh[:, None, :], wu, wd, gate[:, None, :]]
    if final_g is not None:
        in_specs.append(pl.BlockSpec((1, d), lambda i, f: (0, 0)))
        args.append(final_g.reshape(1, d))
    return pl.pallas_call(
        functools.partial(_mlp_kernel, final=final_g is not None, nchunk=max(1, d // 512)),
        grid=(m // tm, dff // tf),
        in_specs=in_specs,
        out_specs=pl.BlockSpec((tm, d), lambda i, f: (i, 0)),
        out_shape=jax.ShapeDtypeStruct((m, d), F32),
        scratch_shapes=[pltpu.VMEM((tm, d), BF16)],
        compiler_params=_params("arbitrary", "arbitrary"),
        name="mlp_final" if final_g is not None else "mlp",
    )(*args)


def _glu_kernel(x_ref, wa_ref, wb_ref, o_ref):
    x = x_ref[...]
    o_ref[...] = _dot(x, wa_ref[...]) * jax.nn.sigmoid(_dot(x, wb_ref[...]))


def _glu(x, w, tm):
    m, k = x.shape
    d = w.shape[1] // 2
    tn = _tile(d, 512)
    nj = d // tn
    return pl.pallas_call(
        _glu_kernel,
        grid=(m // tm, nj),
        in_specs=[pl.BlockSpec((tm, k), lambda i, j: (i, 0)),
                  pl.BlockSpec((k, tn), lambda i, j: (0, j)),
                  pl.BlockSpec((k, tn), lambda i, j: (0, j + nj))],
        out_specs=pl.BlockSpec((tm, tn), lambda i, j: (i, j)),
        out_shape=jax.ShapeDtypeStruct((m, d), F32),
        compiler_params=_params("arbitrary", "arbitrary"),
        name="pointwise_glu",
    )(x, w, w)


def _conv_kernel(u_ref, prev_ref, st_ref, dw_ref, dwb_ref, lng_ref, lnb_ref, y_ref, ns_ref, up, shifted, yacc,
                 *, ts, lane_chunk, row_chunk):
    ti = pl.program_id(1)
    d = u_ref.shape[1]
    off = CONV_HALO - (CONV_WIDTH - 1)

    @pl.when(ti == 0)
    def _():
        up[0:CONV_HALO, :] = st_ref[...]

    @pl.when(ti > 0)
    def _():
        up[0:CONV_HALO, :] = prev_ref[...]

    up[CONV_HALO:CONV_HALO + ts, :] = u_ref[...]

    def lane_body(c, carry):
        cols = pl.ds(pl.multiple_of(c * lane_chunk, lane_chunk), lane_chunk)
        dw = dw_ref[:, cols]
        bias = dwb_ref[:, cols]
        nrow = CONV_HALO + ts - SUBLANE
        for b in range(1, SUBLANE):
            shifted[b - 1, 0:nrow, :] = up[b:b + nrow, cols]
        for r0 in range(0, ts, row_chunk):
            acc = None
            for w in range(CONV_WIDTH):
                a, b = divmod(off + w, SUBLANE)
                r = r0 + a * SUBLANE
                src = up[r:r + row_chunk, cols] if b == 0 else shifted[b - 1, r:r + row_chunk, :]
                term = src * dw[w:w + 1, :]
                acc = term if acc is None else acc + term
            yacc[r0:r0 + row_chunk, cols] = acc + bias
        return carry

    lax.fori_loop(0, d // lane_chunk, lane_body, 0)
    ln_rows = _tile(ts, 32)
    for r0 in range(0, ts, ln_rows):
        y = yacc[r0:r0 + ln_rows, :]
        yc = y - jnp.mean(y, axis=-1, keepdims=True)
        var = jnp.mean(yc * yc, axis=-1, keepdims=True)
        yn = yc * lax.rsqrt(var + NORM_EPS) * lng_ref[...] + lnb_ref[...]
        y_ref[r0:r0 + ln_rows, :] = (yn * jax.nn.sigmoid(yn)).astype(y_ref.dtype)

    @pl.when(ti == pl.num_programs(1) - 1)
    def _():
        ns_ref[...] = up[CONV_HALO + ts - (CONV_WIDTH - 1):CONV_HALO + ts, :]


def _conv_ln_swish(u, state, dw, dw_b, ln_g, ln_b, b, s_len):
    m, d = u.shape
    ts = _tile(s_len, 256)
    assert ts % CONV_HALO == 0 and s_len >= CONV_WIDTH - 1
    nt = s_len // ts
    st = jnp.pad(state, ((0, 0), (CONV_HALO - (CONV_WIDTH - 1), 0), (0, 0)))
    hpb = ts // CONV_HALO
    vec = pl.BlockSpec((1, d), lambda i, t: (0, 0))
    lane_chunk = _tile(d, 256)
    y, ns = pl.pallas_call(
        functools.partial(_conv_kernel, ts=ts, lane_chunk=lane_chunk, row_chunk=_tile(ts, 128)),
        grid=(b, nt),
        in_specs=[pl.BlockSpec((ts, d), lambda i, t: (i * nt + t, 0)),
                  pl.BlockSpec((CONV_HALO, d), lambda i, t: (jnp.maximum((i * nt + t) * hpb - 1, 0), 0)),
                  pl.BlockSpec((None, CONV_HALO, d), lambda i, t: (i, 0, 0)),
                  pl.BlockSpec((CONV_WIDTH, d), lambda i, t: (0, 0)),
                  vec, vec, vec],
        out_specs=[pl.BlockSpec((ts, d), lambda i, t: (i * nt + t, 0)),
                   pl.BlockSpec((None, CONV_WIDTH - 1, d), lambda i, t: (i, 0, 0))],
        out_shape=[jax.ShapeDtypeStruct((m, d), BF16),
                   jax.ShapeDtypeStruct((b, CONV_WIDTH - 1, d), F32)],
        scratch_shapes=[pltpu.VMEM((CONV_HALO + ts, d), F32),
                        pltpu.VMEM((SUBLANE - 1, CONV_HALO + ts, lane_chunk), F32),
                        pltpu.VMEM((ts, d), F32)],
        compiler_params=_params("arbitrary", "arbitrary"),
        name="conv_ln_swish",
    )(u, u, st, dw, dw_b.reshape(1, d), ln_g.reshape(1, d), ln_b.reshape(1, d))
    return y, ns


def _rope_tables(pos):
    half = D_HEAD // 2
    inv_freq = jnp.power(ROPE_THETA, -jnp.arange(half, dtype=F32) / half)
    ang = pos.astype(F32)[:, None] * inv_freq[None, :]
    cos, sin = jnp.cos(ang), jnp.sin(ang)
    return jnp.concatenate([cos, cos], axis=-1), jnp.concatenate([-sin, sin], axis=-1)


def _trunk(x3, mod, cache, state, p, wb):
    b, t, d = x3.shape
    m = b * t
    mix_a = d // 2
    mix_b = d - mix_a
    x = x3.reshape(m, d)
    tm = _tile(m, 1024)
    tm_mlp = _tile(m, 512)
    tm_norm = _tile(m, 512) if t % _tile(m, 512) else _tile(t, 256)
    past_len = 0 if cache is None else cache[0].shape[1]
    rope_tabs = _rope_tables(past_len + jnp.arange(t))

    def split_mod(layer):
        return jnp.split(mod[layer], 6, axis=-1)

    sh_m, sc_m, g_m, sh_f, sc_f, g_f = split_mod(0)
    h = _norm_mod(x, p["norm_mix"][0], sc_m, sh_m, t, tm_norm)
    w_in = wb["w_attn_in"]
    qa = _proj(h, w_in, 0, mix_a, BF16, tm, rope_tabs, t)
    ka = _proj(h, w_in, mix_a, mix_a, F32, tm, rope_tabs, t)
    va = _proj(h, w_in, 2 * mix_a, mix_a, F32, tm)
    qb = _proj(h, w_in, 3 * mix_a, mix_b, BF16, tm)
    kb = _proj(h, w_in, 3 * mix_a + mix_b, mix_b, F32, tm, w_resident=True)
    vb = _proj(h, w_in, 3 * mix_a + 2 * mix_b, mix_b, F32, tm)
    lams = [p[n][0].reshape(1, D_HEAD) for n in ("lambda_q1", "lambda_k1", "lambda_q2", "lambda_k2")]
    subln_g = p["diff_subln_g"][0].reshape(1, 2 * D_HEAD)
    lam_init = 0.8 - 0.6 * math.exp(-0.3 * 0)
    if cache is None:
        o_a = _attn_a_self(qa, ka, va, lams, subln_g, b, t, lam_init)
        o_b = _attn_b_self(qb, kb, vb, b, t)
    else:
        ck_a, cv_a, ck_b, cv_b = cache
        o_a = _attn_a_cached(qa, ck_a.reshape(b, past_len, mix_a), cv_a.reshape(b, past_len, mix_a),
                             ka, va, lams, subln_g, b, t, lam_init)
        o_b = _attn_b_cached(qb, ck_b.reshape(b, past_len, mix_b), cv_b.reshape(b, past_len, mix_b),
                             kb, vb, b, t)
    x = _res_matmul([o_a, o_b], wb["w_attn_out"], x, g_m, t, tm)
    x = _mlp(x, p["norm_mlp"][0], sc_f, sh_f, wb["mlp_up_tiles0"], wb["mlp_down"], 0, g_f, t, tm_mlp)

    sh_m, sc_m, g_m, sh_f, sc_f, g_f = split_mod(1)
    h = _norm_mod(x, p["norm_mix"][1], sc_m, sh_m, t, tm_norm)
    u = _glu(h, wb["conv_pw1"], tm)
    if state is None:
        state = jnp.zeros((b, CONV_WIDTH - 1, d), F32)
    y, new_state = _conv_ln_swish(u, state, p["conv_dw"][0], p["conv_dw_b"][0], p["conv_ln_g"][0],
                                  p["conv_ln_b"][0], b, t)
    x = _res_matmul([y], wb["conv_pw2"], x, g_m, t, tm)
    y_out = _mlp(x, p["norm_mlp"][1], sc_f, sh_f, wb["mlp_up"], wb["mlp_down"], 1, g_f, t, tm_mlp,
                 final_g=p["final_g"])

    h_a, h_b = mix_a // (2 * D_HEAD), mix_b // D_HEAD
    return (y_out.reshape(b, t, d),
            ka.reshape(1, b, t, 2 * h_a, D_HEAD), va.reshape(1, b, t, h_a, 2 * D_HEAD),
            kb.reshape(1, b, t, h_b, D_HEAD), vb.reshape(1, b, t, h_b, D_HEAD),
            new_state[None])


def kernel(x_prompt, x_sample, c_prompt, c_sample, cache_k_diff, cache_v_diff, cache_k_sb, cache_v_sb, state_conv, w_mod, b_mod, norm_mix, norm_mlp, w_attn_in, w_attn_out, lambda_q1, lambda_k1, lambda_q2, lambda_k2, diff_subln_g, conv_pw1, conv_dw, conv_dw_b, conv_ln_g, conv_ln_b, conv_pw2, mlp_up, mlp_down, final_g):
    assert w_mod.shape[0] == 2 and w_attn_in.shape[0] == 1 and conv_pw1.shape[0] == 1
    p = dict(norm_mix=norm_mix, norm_mlp=norm_mlp, lambda_q1=lambda_q1, lambda_k1=lambda_k1,
             lambda_q2=lambda_q2, lambda_k2=lambda_k2, diff_subln_g=diff_subln_g, conv_dw=conv_dw,
             conv_dw_b=conv_dw_b, conv_ln_g=conv_ln_g, conv_ln_b=conv_ln_b, final_g=final_g)
    wb = dict(w_attn_in=w_attn_in[0].astype(BF16), w_attn_out=w_attn_out[0].astype(BF16),
              conv_pw1=conv_pw1[0].astype(BF16), conv_pw2=conv_pw2[0].astype(BF16),
              mlp_up=mlp_up.astype(BF16), mlp_down=mlp_down.astype(BF16),
              mlp_up_tiles0=_mlp_up_tiles(mlp_up[0]))
    nb = c_prompt.shape[0]
    mod = _modulation(jnp.concatenate([c_prompt, c_sample], axis=0), w_mod, b_mod)
    out_p = _trunk(x_prompt, mod[:, :nb], None, None, p, wb)
    cache = (cache_k_diff[0], cache_v_diff[0], cache_k_sb[0], cache_v_sb[0])
    out_s = _trunk(x_sample, mod[:, nb:], cache, state_conv[0], p, wb)
    return (out_p[0], out_s[0]) + out_p[1:] + out_s[1:]
```

```python
import functools
import math

import jax
import jax.numpy as jnp
import numpy as np
from jax import lax
from jax.experimental import pallas as pl
from jax.experimental.pallas import tpu as pltpu

CHUNK = 64
D_HEAD = 128
CONV_WIDTH = 31
ROPE_THETA = 10000.0
NORM_EPS = 1e-6
SUBLN_EPS = 1e-5
NEG_INF = -1e30

F32 = jnp.float32
BF16 = jnp.bfloat16

V7X_VMEM_LIMIT_BYTES = 60000 * 1024
SUBLANE = 8
MLP_FF_TILE = 512
CONV_HALO = -(-(CONV_WIDTH - 1) // SUBLANE) * SUBLANE


def _params(*sem):
    return pltpu.CompilerParams(dimension_semantics=sem, vmem_limit_bytes=V7X_VMEM_LIMIT_BYTES)


def _tile(n, pref):
    t = min(n, pref)
    while n % t:
        t -= 1
    return t


def _dot(a, b):
    return jnp.dot(a, b, preferred_element_type=F32)


def _dot_nt(a, b):
    return lax.dot_general(a, b, (((1,), (1,)), ((), ())), preferred_element_type=F32)


def _stream_vec_spec(t, tm, width, with_col):
    if t % tm == 0:
        spt, tps = 1, t // tm
    else:
        assert tm % t == 0
        spt, tps = tm // t, 1
    if with_col:
        return pl.BlockSpec((spt, 1, width), lambda i, j: (i // tps, 0, j))
    return pl.BlockSpec((spt, 1, width), lambda i, *_: (i // tps, 0, 0))


def _stream_rows(vec_ref, tm):
    spt = vec_ref.shape[0]
    rows = tm // spt
    return [(s, slice(s * rows, (s + 1) * rows)) for s in range(spt)]


def _mod_kernel(c_ref, w_ref, b_ref, o_ref):
    c = c_ref[...]
    cs = (c * jax.nn.sigmoid(c)).astype(BF16)
    o_ref[...] = _dot(cs, w_ref[...].astype(BF16)) + b_ref[...]


def _modulation(c_all, w_mod, b_mod):
    nl, d, n = w_mod.shape
    r = c_all.shape[0]
    tn = _tile(n, 512)
    return pl.pallas_call(
        _mod_kernel,
        grid=(nl, n // tn),
        in_specs=[pl.BlockSpec((r, d), lambda l, j: (0, 0)),
                  pl.BlockSpec((None, d, tn), lambda l, j: (l, 0, j)),
                  pl.BlockSpec((None, 1, tn), lambda l, j: (l, 0, j))],
        out_specs=pl.BlockSpec((None, r, tn), lambda l, j: (l, 0, j)),
        out_shape=jax.ShapeDtypeStruct((nl, r, n), F32),
        compiler_params=_params("arbitrary", "arbitrary"),
        name="modulation",
    )(c_all, w_mod, b_mod.reshape(nl, 1, n))


def _norm_mod_rows(x, g, sc, sh):
    y = x * lax.rsqrt(jnp.mean(x * x, axis=-1, keepdims=True) + NORM_EPS) * g
    return (y * (1.0 + sc) + sh).astype(BF16)


def _norm_mod_kernel(x_ref, g_ref, sc_ref, sh_ref, o_ref):
    for s, rows in _stream_rows(sc_ref, x_ref.shape[0]):
        o_ref[rows, :] = _norm_mod_rows(x_ref[rows, :], g_ref[...], sc_ref[s], sh_ref[s])


def _norm_mod(x, g, sc, sh, t, tm):
    m, d = x.shape
    vec = _stream_vec_spec(t, tm, d, False)
    return pl.pallas_call(
        _norm_mod_kernel,
        grid=(m // tm,),
        in_specs=[pl.BlockSpec((tm, d), lambda i: (i, 0)),
                  pl.BlockSpec((1, d), lambda i: (0, 0)), vec, vec],
        out_specs=pl.BlockSpec((tm, d), lambda i: (i, 0)),
        out_shape=jax.ShapeDtypeStruct((m, d), BF16),
        compiler_params=_params("arbitrary"),
        name="norm_mod",
    )(x, g.reshape(1, d), sc[:, None, :], sh[:, None, :])


def _proj_kernel(*refs, rope):
    if rope:
        x_ref, w_ref, cos_ref, sin_ref, o_ref = refs
    else:
        x_ref, w_ref, o_ref = refs
    acc = _dot(x_ref[...], w_ref[...])
    if rope:
        cos, sin = cos_ref[...], sin_ref[...]
        for h in range(acc.shape[1] // D_HEAD):
            sl = slice(h * D_HEAD, (h + 1) * D_HEAD)
            xh = acc[:, sl]
            o_ref[:, sl] = (xh * cos + pltpu.roll(xh, D_HEAD // 2, 1) * sin).astype(o_ref.dtype)
    else:
        o_ref[...] = acc.astype(o_ref.dtype)


def _proj(x, w, col_off, n, out_dtype, tm, rope_tabs=None, t=None):
    m, k = x.shape
    tn = _tile(n, 1024)
    assert col_off % tn == 0
    joff = col_off // tn

    def spec(shape, fn):
        return pl.BlockSpec(shape, lambda j, i: fn(i, j))

    in_specs = [spec((tm, k), lambda i, j: (i, 0)), spec((k, tn), lambda i, j: (0, j + joff))]
    args = [x, w]
    if rope_tabs is not None:
        cos, sin = rope_tabs
        if t % tm == 0:
            nblk = t // tm
            tab_map = lambda i, j: (i % nblk, 0)
        else:
            assert tm % t == 0
            cos, sin = jnp.tile(cos, (tm // t, 1)), jnp.tile(sin, (tm // t, 1))
            tab_map = lambda i, j: (0, 0)
        in_specs += [spec((tm, D_HEAD), tab_map), spec((tm, D_HEAD), tab_map)]
        args += [cos, sin]
    return pl.pallas_call(
        functools.partial(_proj_kernel, rope=rope_tabs is not None),
        grid=(n // tn, m // tm),
        in_specs=in_specs,
        out_specs=spec((tm, tn), lambda i, j: (i, j)),
        out_shape=jax.ShapeDtypeStruct((m, n), out_dtype),
        compiler_params=_params("arbitrary", "arbitrary"),
        name="proj_rope" if rope_tabs is not None else "proj",
    )(*args)


class _Masks:
    def __init__(self):
        self._cache = {}

    def _get(self, kind, q0, tq, k0, n):
        qp = np.arange(q0, q0 + tq)[:, None]
        kp = np.arange(k0, k0 + n)[None, :]
        vis = (kp // CHUNK <= qp // CHUNK) if kind == "chunk" else (kp < qp)
        if vis.all():
            return None
        assert vis.any()
        aligned = q0 % CHUNK == 0 and k0 % CHUNK == 0
        key = (kind, q0 - k0, tq, n) if (aligned or kind == "before") else (kind, q0, k0, tq, n)
        if key not in self._cache:
            row = lax.broadcasted_iota(jnp.int32, (tq, n), 0) + q0
            col = lax.broadcasted_iota(jnp.int32, (tq, n), 1) + k0
            self._cache[key] = (col // CHUNK <= row // CHUNK) if kind == "chunk" else (col < row)
        return self._cache[key]

    def chunk(self, q0, tq, k0, n):
        return self._get("chunk", q0, tq, k0, n)

    def before(self, q0, tq, k0, n):
        return self._get("before", q0, tq, k0, n)


def _lambda(lq1, lk1, lq2, lk2, lam_init):
    return (jnp.exp(jnp.sum(lq1[...] * lk1[...], keepdims=True))
            - jnp.exp(jnp.sum(lq2[...] * lk2[...], keepdims=True)) + lam_init)


def _diff_scores(q0, q1, segs):
    scale = 1.0 / math.sqrt(D_HEAD)
    s0s, s1s = [], []
    for k0, k1, _, mask in segs:
        s0 = _dot_nt(q0, k0) * scale
        s1 = _dot_nt(q1, k1) * scale
        if mask is not None:
            s0 = jnp.where(mask, s0, NEG_INF)
            s1 = jnp.where(mask, s1, NEG_INF)
        s0s.append(s0)
        s1s.append(s1)
    return s0s, s1s


def _diff_output(s0s, s1s, segs, lam, subln_g, out_scale):
    m0 = functools.reduce(jnp.maximum, [jnp.max(s, axis=1, keepdims=True) for s in s0s])
    m1 = functools.reduce(jnp.maximum, [jnp.max(s, axis=1, keepdims=True) for s in s1s])
    p0s = [jnp.exp(s - m0) for s in s0s]
    p1s = [jnp.exp(s - m1) for s in s1s]
    l0 = functools.reduce(jnp.add, [jnp.sum(p, axis=1, keepdims=True) for p in p0s])
    l1 = functools.reduce(jnp.add, [jnp.sum(p, axis=1, keepdims=True) for p in p1s])
    inv0 = 1.0 / l0
    inv1 = lam / l1
    o = None
    for (_, _, v, _), p0, p1 in zip(segs, p0s, p1s):
        part = _dot((p0 * inv0 - p1 * inv1).astype(BF16), v)
        o = part if o is None else o + part
    o = o * lax.rsqrt(jnp.mean(o * o, axis=-1, keepdims=True) + SUBLN_EPS) * subln_g
    return o * out_scale


def _tri(n):
    j = lax.broadcasted_iota(jnp.int32, (2 * n, n), 0)
    s = lax.broadcasted_iota(jnp.int32, (2 * n, n), 1)
    return (jnp.where(j >= n, j - n, j) >= s).astype(BF16)


def _stick_scores(q, blocks):
    scale = 1.0 / math.sqrt(D_HEAD)
    zs, splits = [], []
    for k, _, mask in blocks:
        z = _dot_nt(q, k) * scale
        sp = jnp.maximum(z, 0.0) + jnp.log(1.0 + jnp.exp(-jnp.abs(z)))
        if mask is not None:
            sp = jnp.where(mask, sp, 0.0)
        hi = sp.astype(BF16)
        lo = (sp - hi.astype(F32)).astype(BF16)
        zs.append(z)
        splits.append(jnp.concatenate([hi, lo], axis=1))
    return zs, splits


def _stick_output(zs, splits, blocks, tris):
    tq = zs[0].shape[0]
    incls = [None] * len(blocks)
    for n in sorted({z.shape[1] for z in zs}):
        idx = [i for i, z in enumerate(zs) if z.shape[1] == n]
        stacked = splits[idx[0]] if len(idx) == 1 else jnp.concatenate([splits[i] for i in idx], axis=0)
        inc = _dot(stacked, tris[n])
        for r, i in enumerate(idx):
            incls[i] = inc[r * tq:(r + 1) * tq, :]
    carry = None
    o = None
    for i in reversed(range(len(blocks))):
        _, v, mask = blocks[i]
        expo = zs[i] - incls[i]
        if carry is not None:
            expo = expo - carry
        a = jnp.exp(expo)
        if mask is not None:
            a = jnp.where(mask, a, 0.0)
        part = _dot(a.astype(BF16), v)
        o = part if o is None else o + part
        total = incls[i][:, 0:1]
        carry = total if carry is None else carry + total
    return o


def _attn_a_self_kernel(q_ref, k_ref, v_ref, lq1, lk1, lq2, lk2, g_ref, o_ref, kb, vb, *, tq, lam_init):
    s_len = q_ref.shape[0]
    kb[...] = k_ref[...].astype(BF16)
    vb[...] = v_ref[...].astype(BF16)
    lam = _lambda(lq1, lk1, lq2, lk2, lam_init)
    g = g_ref[...]
    masks = _Masks()
    nq = s_len // tq
    pending = None
    for qi in range(nq + 1):
        stage1 = None
        if qi < nq:
            r0 = qi * tq
            rows = slice(r0, r0 + tq)
            segs = []
            if qi > 0:
                segs.append((kb[0:r0, :D_HEAD], kb[0:r0, D_HEAD:], vb[0:r0, :], masks.chunk(r0, tq, 0, r0)))
            segs.append((kb[rows, :D_HEAD], kb[rows, D_HEAD:], vb[rows, :], masks.chunk(r0, tq, r0, tq)))
            stage1 = (rows, segs) + _diff_scores(q_ref[rows, :D_HEAD], q_ref[rows, D_HEAD:], segs)
        if pending is not None:
            prows, psegs, s0s, s1s = pending
            o_ref[prows, :] = _diff_output(s0s, s1s, psegs, lam, g, 1.0 - lam_init).astype(o_ref.dtype)
        pending = stage1


def _attn_a_self(q, k, v, lams, subln_g, b, s_len, lam_init):
    m, width = q.shape
    hw = 2 * D_HEAD
    tq = _tile(s_len, 256)
    assert tq % CHUNK == 0
    blk = pl.BlockSpec((s_len, hw), lambda i, h: (i, h))
    vec = pl.BlockSpec((1, D_HEAD), lambda i, h: (0, 0))
    return pl.pallas_call(
        functools.partial(_attn_a_self_kernel, tq=tq, lam_init=lam_init),
        grid=(b, width // hw),
        in_specs=[blk, blk, blk, vec, vec, vec, vec, pl.BlockSpec((1, hw), lambda i, h: (0, 0))],
        out_specs=blk,
        out_shape=jax.ShapeDtypeStruct((m, width), BF16),
        scratch_shapes=[pltpu.VMEM((s_len, hw), BF16), pltpu.VMEM((s_len, hw), BF16)],
        compiler_params=_params("arbitrary", "arbitrary"),
        name="diff_attn_self",
    )(q, k, v, *lams, subln_g)


def _attn_b_self_kernel(q_ref, k_ref, v_ref, o_ref, kb, vb, *, tq):
    s_len = q_ref.shape[0]
    kb[...] = k_ref[...].astype(BF16)
    vb[...] = v_ref[...].astype(BF16)
    tris = {tq: _tri(tq)}
    masks = _Masks()
    pending = None
    nq = s_len // tq
    work = [(hh, qi) for hh in range(q_ref.shape[1] // D_HEAD) for qi in range(nq)]
    for item in work + [None]:
        stage1 = None
        if item is not None:
            hh, qi = item
            cols = slice(hh * D_HEAD, (hh + 1) * D_HEAD)
            r0 = qi * tq
            rows = slice(r0, r0 + tq)
            blocks = []
            for j in range(qi + 1):
                keys = slice(j * tq, (j + 1) * tq)
                blocks.append((kb[keys, cols], vb[keys, cols], masks.before(r0, tq, j * tq, tq)))
            stage1 = (rows, cols, blocks) + _stick_scores(q_ref[rows, cols], blocks)
        if pending is not None:
            prows, pcols, pblocks, zs, splits = pending
            o_ref[prows, pcols] = _stick_output(zs, splits, pblocks, tris).astype(o_ref.dtype)
        pending = stage1


def _attn_b_self(q, k, v, b, s_len):
    m, width = q.shape
    tq = _tile(s_len, 256)
    hw = _tile(width, 2 * D_HEAD)
    blk = pl.BlockSpec((s_len, hw), lambda i, h: (i, h))
    return pl.pallas_call(
        functools.partial(_attn_b_self_kernel, tq=tq),
        grid=(b, width // hw),
        in_specs=[blk, blk, blk],
        out_specs=blk,
        out_shape=jax.ShapeDtypeStruct((m, width), BF16),
        scratch_shapes=[pltpu.VMEM((s_len, hw), BF16), pltpu.VMEM((s_len, hw), BF16)],
        compiler_params=_params("arbitrary", "arbitrary"),
        name="stick_attn_self",
    )(q, k, v)


def _attn_a_cached_kernel(q_ref, kc_ref, vc_ref, kn_ref, vn_ref, lq1, lk1, lq2, lk2, g_ref, o_ref, *, lam_init):
    t = q_ref.shape[0]
    p = kc_ref.shape[0]
    kc = kc_ref[...].astype(BF16)
    vc = vc_ref[...].astype(BF16)
    kn = kn_ref[...].astype(BF16)
    vn = vn_ref[...].astype(BF16)
    lam = _lambda(lq1, lk1, lq2, lk2, lam_init)
    masks = _Masks()
    segs = [(kc[:, :D_HEAD], kc[:, D_HEAD:], vc, masks.chunk(p, t, 0, p)),
            (kn[:, :D_HEAD], kn[:, D_HEAD:], vn, masks.chunk(p, t, p, t))]
    s0s, s1s = _diff_scores(q_ref[:, :D_HEAD], q_ref[:, D_HEAD:], segs)
    o_ref[...] = _diff_output(s0s, s1s, segs, lam, g_ref[...], 1.0 - lam_init).astype(o_ref.dtype)


def _attn_a_cached(q, kc, vc, kn, vn, lams, subln_g, b, t, lam_init):
    m, width = q.shape
    p = kc.shape[1]
    hw = 2 * D_HEAD
    blk = pl.BlockSpec((t, hw), lambda i, h: (i, h))
    cblk = pl.BlockSpec((None, p, hw), lambda i, h: (i, 0, h))
    vec = pl.BlockSpec((1, D_HEAD), lambda i, h: (0, 0))
    return pl.pallas_call(
        functools.partial(_attn_a_cached_kernel, lam_init=lam_init),
        grid=(b, width // hw),
        in_specs=[blk, cblk, cblk, blk, blk, vec, vec, vec, vec, pl.BlockSpec((1, hw), lambda i, h: (0, 0))],
        out_specs=blk,
        out_shape=jax.ShapeDtypeStruct((m, width), BF16),
        compiler_params=_params("arbitrary", "arbitrary"),
        name="diff_attn_cached",
    )(q, kc, vc, kn, vn, *lams, subln_g)


def _attn_b_cached_kernel(q_ref, kc_ref, vc_ref, kn_ref, vn_ref, o_ref, *, tk):
    t = q_ref.shape[0]
    p = kc_ref.shape[0]
    kc = kc_ref[...].astype(BF16)
    vc = vc_ref[...].astype(BF16)
    masks = _Masks()
    blocks = []
    for j in range(p // tk):
        keys = slice(j * tk, (j + 1) * tk)
        blocks.append((kc[keys, :], vc[keys, :], masks.before(p, t, j * tk, tk)))
    blocks.append((kn_ref[...].astype(BF16), vn_ref[...].astype(BF16), masks.before(p, t, p, t)))
    tris = {n: _tri(n) for n in {tk, t}}
    zs, splits = _stick_scores(q_ref[...], blocks)
    o_ref[...] = _stick_output(zs, splits, blocks, tris).astype(o_ref.dtype)


def _attn_b_cached(q, kc, vc, kn, vn, b, t):
    m, width = q.shape
    p = kc.shape[1]
    tk = _tile(p, 256)
    blk = pl.BlockSpec((t, D_HEAD), lambda i, h: (i, h))
    cblk = pl.BlockSpec((None, p, D_HEAD), lambda i, h: (i, 0, h))
    return pl.pallas_call(
        functools.partial(_attn_b_cached_kernel, tk=tk),
        grid=(b, width // D_HEAD),
        in_specs=[blk, cblk, cblk, blk, blk],
        out_specs=blk,
        out_shape=jax.ShapeDtypeStruct((m, width), BF16),
        compiler_params=_params("arbitrary", "arbitrary"),
        name="stick_attn_cached",
    )(q, kc, vc, kn, vn)


def _res_mm_kernel(*refs, nx):
    xs, ws = refs[:nx], refs[nx:2 * nx]
    xres_ref, gate_ref, o_ref = refs[2 * nx:]
    acc = _dot(xs[0][...], ws[0][...])
    for x_ref, w_ref in zip(xs[1:], ws[1:]):
        acc = acc + _dot(x_ref[...], w_ref[...])
    for s, rows in _stream_rows(gate_ref, o_ref.shape[0]):
        o_ref[rows, :] = xres_ref[rows, :] + gate_ref[s] * acc[rows, :]


def _res_matmul(xs, w, xres, gate, t, tm, tn_pref=512):
    m, n = xres.shape
    kp = xs[0].shape[1]
    nx = len(xs)
    assert all(x.shape == (m, kp) for x in xs) and w.shape == (nx * kp, n)
    tn = _tile(n, tn_pref)
    in_specs = [pl.BlockSpec((tm, kp), lambda i, j: (i, 0)) for _ in xs]
    in_specs += [pl.BlockSpec((kp, tn), functools.partial(lambda i, j, p: (p, j), p=p)) for p in range(nx)]
    in_specs += [pl.BlockSpec((tm, tn), lambda i, j: (i, j)), _stream_vec_spec(t, tm, tn, True)]
    return pl.pallas_call(
        functools.partial(_res_mm_kernel, nx=nx),
        grid=(m // tm, n // tn),
        in_specs=in_specs,
        out_specs=pl.BlockSpec((tm, tn), lambda i, j: (i, j)),
        out_shape=jax.ShapeDtypeStruct((m, n), F32),
        compiler_params=_params("arbitrary", "arbitrary"),
        name="res_matmul",
    )(*xs, *([w] * nx), xres, gate[:, None, :])


def _mlp_kernel(*refs, final, nchunk):
    if final:
        x_hbm, ng_ref, sc_ref, sh_ref, wu_ref, wd_ref, g_ref, fg_ref, o_ref, h_scr, x_ref, x_sem = refs
    else:
        x_hbm, ng_ref, sc_ref, sh_ref, wu_ref, wd_ref, g_ref, o_ref, h_scr, x_ref, x_sem = refs
    f = pl.program_id(1)
    tm = o_ref.shape[0]
    streams = _stream_rows(g_ref, tm)

    @pl.when(f == 0)
    def _():
        row0 = pl.multiple_of(pl.program_id(0) * tm, tm)
        copy = pltpu.make_async_copy(x_hbm.at[pl.ds(row0, tm), :], x_ref, x_sem)
        copy.start()
        copy.wait()
        for s, rows in streams:
            h_scr[rows, :] = _norm_mod_rows(x_ref[rows, :], ng_ref[...], sc_ref[s], sh_ref[s])

    hid = _dot(h_scr[...], wu_ref[...])
    hid = jnp.square(jnp.maximum(hid, 0.0)).astype(BF16)
    d = o_ref.shape[1]
    cols = [slice(c * (d // nchunk), (c + 1) * (d // nchunk)) for c in range(nchunk)]

    @pl.when(f == 0)
    def _():
        for sl in cols:
            o_ref[:, sl] = _dot(hid, wd_ref[:, sl])

    @pl.when(f > 0)
    def _():
        for sl in cols:
            o_ref[:, sl] += _dot(hid, wd_ref[:, sl])

    @pl.when(f == pl.num_programs(1) - 1)
    def _():
        for s, rows in streams:
            x = x_ref[rows, :] + g_ref[s] * o_ref[rows, :]
            if final:
                x = x * lax.rsqrt(jnp.mean(x * x, axis=-1, keepdims=True) + NORM_EPS) * fg_ref[...]
            o_ref[rows, :] = x


def _mlp(x, norm_g, sc, sh, wu, wd, layer, gate, t, tm, final_g=None):
    m, d = x.shape
    dff = wd.shape[1]
    tf = _tile(dff, MLP_FF_TILE)
    vec = _stream_vec_spec(t, tm, d, False)
    in_specs = [pl.BlockSpec(memory_space=pl.ANY),
                pl.BlockSpec((1, d), lambda i, f: (0, 0)), vec, vec,
                pl.BlockSpec((None, d, tf), lambda i, f: (layer, 0, f)),
                pl.BlockSpec((None, tf, d), lambda i, f: (layer, f, 0)),
                vec]
    args = [x, norm_g.reshape(1, d), sc[:, None, :], sh[:, None, :], wu, wd, gate[:, None, :]]
    if final_g is not None:
        in_specs.append(pl.BlockSpec((1, d), lambda i, f: (0, 0)))
        args.append(final_g.reshape(1, d))
    return pl.pallas_call(
        functools.partial(_mlp_kernel, final=final_g is not None, nchunk=max(1, d // 512)),
        grid=(m // tm, dff // tf),
        in_specs=in_specs,
        out_specs=pl.BlockSpec((tm, d), lambda i, f: (i, 0)),
        out_shape=jax.ShapeDtypeStruct((m, d), F32),
        scratch_shapes=[pltpu.VMEM((tm, d), BF16), pltpu.VMEM((tm, d), F32), pltpu.SemaphoreType.DMA(())],
        compiler_params=_params("arbitrary", "arbitrary"),
        name="mlp_final" if final_g is not None else "mlp",
    )(*args)


def _glu_kernel(x_ref, wa_ref, wb_ref, o_ref):
    x = x_ref[...]
    o_ref[...] = _dot(x, wa_ref[...]) * jax.nn.sigmoid(_dot(x, wb_ref[...]))


def _glu(x, w, tm):
    m, k = x.shape
    d = w.shape[1] // 2
    tn = _tile(d, 512)
    nj = d // tn
    return pl.pallas_call(
        _glu_kernel,
        grid=(m // tm, nj),
        in_specs=[pl.BlockSpec((tm, k), lambda i, j: (i, 0)),
                  pl.BlockSpec((k, tn), lambda i, j: (0, j)),
                  pl.BlockSpec((k, tn), lambda i, j: (0, j + nj))],
        out_specs=pl.BlockSpec((tm, tn), lambda i, j: (i, j)),
        out_shape=jax.ShapeDtypeStruct((m, d), F32),
        compiler_params=_params("arbitrary", "arbitrary"),
        name="pointwise_glu",
    )(x, w, w)


def _conv_kernel(u_ref, prev_ref, st_ref, dw_ref, dwb_ref, lng_ref, lnb_ref, y_ref, ns_ref, up, shifted, yacc,
                 *, ts, lane_chunk, row_chunk):
    ti = pl.program_id(1)
    d = u_ref.shape[1]
    off = CONV_HALO - (CONV_WIDTH - 1)

    @pl.when(ti == 0)
    def _():
        up[0:CONV_HALO, :] = st_ref[...]

    @pl.when(ti > 0)
    def _():
        up[0:CONV_HALO, :] = prev_ref[...]

    up[CONV_HALO:CONV_HALO + ts, :] = u_ref[...]

    def lane_body(c, carry):
        cols = pl.ds(pl.multiple_of(c * lane_chunk, lane_chunk), lane_chunk)
        dw = dw_ref[:, cols]
        bias = dwb_ref[:, cols]
        nrow = CONV_HALO + ts - SUBLANE
        for b in range(1, SUBLANE):
            shifted[b - 1, 0:nrow, :] = up[b:b + nrow, cols]
        for r0 in range(0, ts, row_chunk):
            acc = None
            for w in range(CONV_WIDTH):
                a, b = divmod(off + w, SUBLANE)
                r = r0 + a * SUBLANE
                src = up[r:r + row_chunk, cols] if b == 0 else shifted[b - 1, r:r + row_chunk, :]
                term = src * dw[w:w + 1, :]
                acc = term if acc is None else acc + term
            yacc[r0:r0 + row_chunk, cols] = acc + bias
        return carry

    lax.fori_loop(0, d // lane_chunk, lane_body, 0)
    ln_rows = _tile(ts, 32)
    for r0 in range(0, ts, ln_rows):
        y = yacc[r0:r0 + ln_rows, :]
        yc = y - jnp.mean(y, axis=-1, keepdims=True)
        var = jnp.mean(yc * yc, axis=-1, keepdims=True)
        yn = yc * lax.rsqrt(var + NORM_EPS) * lng_ref[...] + lnb_ref[...]
        y_ref[r0:r0 + ln_rows, :] = (yn * jax.nn.sigmoid(yn)).astype(y_ref.dtype)

    @pl.when(ti == pl.num_programs(1) - 1)
    def _():
        ns_ref[...] = up[CONV_HALO + ts - (CONV_WIDTH - 1):CONV_HALO + ts, :]


def _conv_ln_swish(u, state, dw, dw_b, ln_g, ln_b, b, s_len):
    m, d = u.shape
    ts = _tile(s_len, 256)
    assert ts % CONV_HALO == 0 and s_len >= CONV_WIDTH - 1
    nt = s_len // ts
    st = jnp.pad(state, ((0, 0), (CONV_HALO - (CONV_WIDTH - 1), 0), (0, 0)))
    hpb = ts // CONV_HALO
    vec = pl.BlockSpec((1, d), lambda i, t: (0, 0))
    lane_chunk = _tile(d, 256)
    y, ns = pl.pallas_call(
        functools.partial(_conv_kernel, ts=ts, lane_chunk=lane_chunk, row_chunk=_tile(ts, 128)),
        grid=(b, nt),
        in_specs=[pl.BlockSpec((ts, d), lambda i, t: (i * nt + t, 0)),
                  pl.BlockSpec((CONV_HALO, d), lambda i, t: (jnp.maximum((i * nt + t) * hpb - 1, 0), 0)),
                  pl.BlockSpec((None, CONV_HALO, d), lambda i, t: (i, 0, 0)),
                  pl.BlockSpec((CONV_WIDTH, d), lambda i, t: (0, 0)),
                  vec, vec, vec],
        out_specs=[pl.BlockSpec((ts, d), lambda i, t: (i * nt + t, 0)),
                   pl.BlockSpec((None, CONV_WIDTH - 1, d), lambda i, t: (i, 0, 0))],
        out_shape=[jax.ShapeDtypeStruct((m, d), BF16),
                   jax.ShapeDtypeStruct((b, CONV_WIDTH - 1, d), F32)],
        scratch_shapes=[pltpu.VMEM((CONV_HALO + ts, d), F32),
                        pltpu.VMEM((SUBLANE - 1, CONV_HALO + ts, lane_chunk), F32),
                        pltpu.VMEM((ts, d), F32)],
        compiler_params=_params("arbitrary", "arbitrary"),
        name="conv_ln_swish",
    )(u, u, st, dw, dw_b.reshape(1, d), ln_g.reshape(1, d), ln_b.reshape(1, d))
    return y, ns


def _rope_tables(pos):
    half = D_HEAD // 2
    inv_freq = jnp.power(ROPE_THETA, -jnp.arange(half, dtype=F32) / half)
    ang = pos.astype(F32)[:, None] * inv_freq[None, :]
    cos, sin = jnp.cos(ang), jnp.sin(ang)
    return jnp.concatenate([cos, cos], axis=-1), jnp.concatenate([-sin, sin], axis=-1)


def _trunk(x3, mod, cache, state, p, wb):
    b, t, d = x3.shape
    m = b * t
    mix_a = d // 2
    mix_b = d - mix_a
    x = x3.reshape(m, d)
    tm = _tile(m, 1024)
    tm_mlp = _tile(m, 512)
    tm_norm = _tile(m, 512) if t % _tile(m, 512) else _tile(t, 256)
    past_len = 0 if cache is None else cache[0].shape[1]
    rope_tabs = _rope_tables(past_len + jnp.arange(t))

    def split_mod(layer):
        return jnp.split(mod[layer], 6, axis=-1)

    sh_m, sc_m, g_m, sh_f, sc_f, g_f = split_mod(0)
    h = _norm_mod(x, p["norm_mix"][0], sc_m, sh_m, t, tm_norm)
    w_in = wb["w_attn_in"]
    qa = _proj(h, w_in, 0, mix_a, BF16, tm, rope_tabs, t)
    ka = _proj(h, w_in, mix_a, mix_a, F32, tm, rope_tabs, t)
    va = _proj(h, w_in, 2 * mix_a, mix_a, F32, tm)
    qb = _proj(h, w_in, 3 * mix_a, mix_b, BF16, tm)
    kb = _proj(h, w_in, 3 * mix_a + mix_b, mix_b, F32, tm)
    vb = _proj(h, w_in, 3 * mix_a + 2 * mix_b, mix_b, F32, tm)
    lams = [p[n][0].reshape(1, D_HEAD) for n in ("lambda_q1", "lambda_k1", "lambda_q2", "lambda_k2")]
    subln_g = p["diff_subln_g"][0].reshape(1, 2 * D_HEAD)
    lam_init = 0.8 - 0.6 * math.exp(-0.3 * 0)
    if cache is None:
        o_a = _attn_a_self(qa, ka, va, lams, subln_g, b, t, lam_init)
        o_b = _attn_b_self(qb, kb, vb, b, t)
    else:
        ck_a, cv_a, ck_b, cv_b = cache
        o_a = _attn_a_cached(qa, ck_a.reshape(b, past_len, mix_a), cv_a.reshape(b, past_len, mix_a),
                             ka, va, lams, subln_g, b, t, lam_init)
        o_b = _attn_b_cached(qb, ck_b.reshape(b, past_len, mix_b), cv_b.reshape(b, past_len, mix_b),
                             kb, vb, b, t)
    x = _res_matmul([o_a, o_b], wb["w_attn_out"], x, g_m, t, tm)
    x = _mlp(x, p["norm_mlp"][0], sc_f, sh_f, wb["mlp_up"], wb["mlp_down"], 0, g_f, t, tm_mlp)

    sh_m, sc_m, g_m, sh_f, sc_f, g_f = split_mod(1)
    h = _norm_mod(x, p["norm_mix"][1], sc_m, sh_m, t, tm_norm)
    u = _glu(h, wb["conv_pw1"], tm)
    if state is None:
        state = jnp.zeros((b, CONV_WIDTH - 1, d), F32)
    y, new_state = _conv_ln_swish(u, state, p["conv_dw"][0], p["conv_dw_b"][0], p["conv_ln_g"][0],
                                  p["conv_ln_b"][0], b, t)
    x = _res_matmul([y], wb["conv_pw2"], x, g_m, t, tm)
    y_out = _mlp(x, p["norm_mlp"][1], sc_f, sh_f, wb["mlp_up"], wb["mlp_down"], 1, g_f, t, tm_mlp,
                 final_g=p["final_g"])

    h_a, h_b = mix_a // (2 * D_HEAD), mix_b // D_HEAD
    return (y_out.reshape(b, t, d),
            ka.reshape(1, b, t, 2 * h_a, D_HEAD), va.reshape(1, b, t, h_a, 2 * D_HEAD),
            kb.reshape(1, b, t, h_b, D_HEAD), vb.reshape(1, b, t, h_b, D_HEAD),
            new_state[None])


def kernel(x_prompt, x_sample, c_prompt, c_sample, cache_k_diff, cache_v_diff, cache_k_sb, cache_v_sb, state_conv, w_mod, b_mod, norm_mix, norm_mlp, w_attn_in, w_attn_out, lambda_q1, lambda_k1, lambda_q2, lambda_k2, diff_subln_g, conv_pw1, conv_dw, conv_dw_b, conv_ln_g, conv_ln_b, conv_pw2, mlp_up, mlp_down, final_g):
    assert w_mod.shape[0] == 2 and w_attn_in.shape[0] == 1 and conv_pw1.shape[0] == 1
    p = dict(norm_mix=norm_mix, norm_mlp=norm_mlp, lambda_q1=lambda_q1, lambda_k1=lambda_k1,
             lambda_q2=lambda_q2, lambda_k2=lambda_k2, diff_subln_g=diff_subln_g, conv_dw=conv_dw,
             conv_dw_b=conv_dw_b, conv_ln_g=conv_ln_g, conv_ln_b=conv_ln_b, final_g=final_g)
    wb = dict(w_attn_in=w_attn_in[0].astype(BF16), w_attn_out=w_attn_out[0].astype(BF16),
              conv_pw1=conv_pw1[0].astype(BF16), conv_pw2=conv_pw2[0].astype(BF16),
              mlp_up=mlp_up.astype(BF16), mlp_down=mlp_down.astype(BF16))
    nb = c_prompt.shape[0]
    mod = _modulation(jnp.concatenate([c_prompt, c_sample], axis=0), w_mod, b_mod)
    out_p = _trunk(x_prompt, mod[:, :nb], None, None, p, wb)
    cache = (cache_k_diff[0], cache_v_diff[0], cache_k_sb[0], cache_v_sb[0])
    out_s = _trunk(x_sample, mod[:, nb:], cache, state_conv[0], p, wb)
    return (out_p[0], out_s[0]) + out_p[1:] + out_s[1:]
```

```python
import functools
import math

import jax
import jax.numpy as jnp
import numpy as np
from jax import lax
from jax.experimental import pallas as pl
from jax.experimental.pallas import tpu as pltpu

CHUNK = 64
D_HEAD = 128
CONV_WIDTH = 31
ROPE_THETA = 10000.0
NORM_EPS = 1e-6
SUBLN_EPS = 1e-5
NEG_INF = -1e30

F32 = jnp.float32
BF16 = jnp.bfloat16

V7X_VMEM_LIMIT_BYTES = 60000 * 1024
SUBLANE = 8
MLP_FF_TILE = 512
CONV_HALO = -(-(CONV_WIDTH - 1) // SUBLANE) * SUBLANE


def _params(*sem):
    return pltpu.CompilerParams(dimension_semantics=sem, vmem_limit_bytes=V7X_VMEM_LIMIT_BYTES)


def _tile(n, pref):
    t = min(n, pref)
    while n % t:
        t -= 1
    return t


def _dot(a, b):
    return jnp.dot(a, b, preferred_element_type=F32)


def _dot_nt(a, b):
    return lax.dot_general(a, b, (((1,), (1,)), ((), ())), preferred_element_type=F32)


def _stream_vec_spec(t, tm, width, with_col):
    if t % tm == 0:
        spt, tps = 1, t // tm
    else:
        assert tm % t == 0
        spt, tps = tm // t, 1
    if with_col:
        return pl.BlockSpec((spt, 1, width), lambda i, j: (i // tps, 0, j))
    return pl.BlockSpec((spt, 1, width), lambda i, *_: (i // tps, 0, 0))


def _stream_rows(vec_ref, tm):
    spt = vec_ref.shape[0]
    rows = tm // spt
    return [(s, slice(s * rows, (s + 1) * rows)) for s in range(spt)]


def _mod_kernel(c_ref, w_ref, b_ref, o_ref):
    c = c_ref[...]
    cs = (c * jax.nn.sigmoid(c)).astype(BF16)
    o_ref[...] = _dot(cs, w_ref[...].astype(BF16)) + b_ref[...]


def _modulation(c_all, w_mod, b_mod):
    nl, d, n = w_mod.shape
    r = c_all.shape[0]
    tn = _tile(n, 512)
    return pl.pallas_call(
        _mod_kernel,
        grid=(nl, n // tn),
        in_specs=[pl.BlockSpec((r, d), lambda l, j: (0, 0)),
                  pl.BlockSpec((None, d, tn), lambda l, j: (l, 0, j)),
                  pl.BlockSpec((None, 1, tn), lambda l, j: (l, 0, j))],
        out_specs=pl.BlockSpec((None, r, tn), lambda l, j: (l, 0, j)),
        out_shape=jax.ShapeDtypeStruct((nl, r, n), F32),
        compiler_params=_params("arbitrary", "arbitrary"),
        name="modulation",
    )(c_all, w_mod, b_mod.reshape(nl, 1, n))


def _norm_mod_rows(x, g, sc, sh):
    y = x * lax.rsqrt(jnp.mean(x * x, axis=-1, keepdims=True) + NORM_EPS) * g
    return (y * (1.0 + sc) + sh).astype(BF16)


def _norm_mod_kernel(x_ref, g_ref, sc_ref, sh_ref, o_ref):
    for s, rows in _stream_rows(sc_ref, x_ref.shape[0]):
        o_ref[rows, :] = _norm_mod_rows(x_ref[rows, :], g_ref[...], sc_ref[s], sh_ref[s])


def _norm_mod(x, g, sc, sh, t, tm):
    m, d = x.shape
    vec = _stream_vec_spec(t, tm, d, False)
    return pl.pallas_call(
        _norm_mod_kernel,
        grid=(m // tm,),
        in_specs=[pl.BlockSpec((tm, d), lambda i: (i, 0)),
                  pl.BlockSpec((1, d), lambda i: (0, 0)), vec, vec],
        out_specs=pl.BlockSpec((tm, d), lambda i: (i, 0)),
        out_shape=jax.ShapeDtypeStruct((m, d), BF16),
        compiler_params=_params("arbitrary"),
        name="norm_mod",
    )(x, g.reshape(1, d), sc[:, None, :], sh[:, None, :])


def _proj_kernel(*refs, rope):
    if rope:
        x_ref, w_ref, cos_ref, sin_ref, o_ref = refs
    else:
        x_ref, w_ref, o_ref = refs
    acc = _dot(x_ref[...], w_ref[...])
    if rope:
        cos, sin = cos_ref[...], sin_ref[...]
        for h in range(acc.shape[1] // D_HEAD):
            sl = slice(h * D_HEAD, (h + 1) * D_HEAD)
            xh = acc[:, sl]
            o_ref[:, sl] = (xh * cos + pltpu.roll(xh, D_HEAD // 2, 1) * sin).astype(o_ref.dtype)
    else:
        o_ref[...] = acc.astype(o_ref.dtype)


def _proj(x, w, col_off, n, out_dtype, tm, rope_tabs=None, t=None):
    m, k = x.shape
    tn = _tile(n, 1024)
    assert col_off % tn == 0
    joff = col_off // tn

    def spec(shape, fn):
        return pl.BlockSpec(shape, lambda j, i: fn(i, j))

    in_specs = [spec((tm, k), lambda i, j: (i, 0)), spec((k, tn), lambda i, j: (0, j + joff))]
    args = [x, w]
    if rope_tabs is not None:
        cos, sin = rope_tabs
        if t % tm == 0:
            nblk = t // tm
            tab_map = lambda i, j: (i % nblk, 0)
        else:
            assert tm % t == 0
            cos, sin = jnp.tile(cos, (tm // t, 1)), jnp.tile(sin, (tm // t, 1))
            tab_map = lambda i, j: (0, 0)
        in_specs += [spec((tm, D_HEAD), tab_map), spec((tm, D_HEAD), tab_map)]
        args += [cos, sin]
    return pl.pallas_call(
        functools.partial(_proj_kernel, rope=rope_tabs is not None),
        grid=(n // tn, m // tm),
        in_specs=in_specs,
        out_specs=spec((tm, tn), lambda i, j: (i, j)),
        out_shape=jax.ShapeDtypeStruct((m, n), out_dtype),
        compiler_params=_params("arbitrary", "arbitrary"),
        name="proj_rope" if rope_tabs is not None else "proj",
    )(*args)


class _Masks:
    def __init__(self):
        self._cache = {}

    def _get(self, kind, q0, tq, k0, n):
        qp = np.arange(q0, q0 + tq)[:, None]
        kp = np.arange(k0, k0 + n)[None, :]
        vis = (kp // CHUNK <= qp // CHUNK) if kind == "chunk" else (kp < qp)
        if vis.all():
            return None
        assert vis.any()
        aligned = q0 % CHUNK == 0 and k0 % CHUNK == 0
        key = (kind, q0 - k0, tq, n) if (aligned or kind == "before") else (kind, q0, k0, tq, n)
        if key not in self._cache:
            row = lax.broadcasted_iota(jnp.int32, (tq, n), 0) + q0
            col = lax.broadcasted_iota(jnp.int32, (tq, n), 1) + k0
            self._cache[key] = (col // CHUNK <= row // CHUNK) if kind == "chunk" else (col < row)
        return self._cache[key]

    def chunk(self, q0, tq, k0, n):
        return self._get("chunk", q0, tq, k0, n)

    def before(self, q0, tq, k0, n):
        return self._get("before", q0, tq, k0, n)


def _lambda(lq1, lk1, lq2, lk2, lam_init):
    return (jnp.exp(jnp.sum(lq1[...] * lk1[...], keepdims=True))
            - jnp.exp(jnp.sum(lq2[...] * lk2[...], keepdims=True)) + lam_init)


def _diff_scores(q0, q1, segs):
    scale = 1.0 / math.sqrt(D_HEAD)
    s0s, s1s = [], []
    for k0, k1, _, mask in segs:
        s0 = _dot_nt(q0, k0) * scale
        s1 = _dot_nt(q1, k1) * scale
        if mask is not None:
            s0 = jnp.where(mask, s0, NEG_INF)
            s1 = jnp.where(mask, s1, NEG_INF)
        s0s.append(s0)
        s1s.append(s1)
    return s0s, s1s


def _diff_output(s0s, s1s, segs, lam, subln_g, out_scale):
    m0 = functools.reduce(jnp.maximum, [jnp.max(s, axis=1, keepdims=True) for s in s0s])
    m1 = functools.reduce(jnp.maximum, [jnp.max(s, axis=1, keepdims=True) for s in s1s])
    p0s = [jnp.exp(s - m0) for s in s0s]
    p1s = [jnp.exp(s - m1) for s in s1s]
    l0 = functools.reduce(jnp.add, [jnp.sum(p, axis=1, keepdims=True) for p in p0s])
    l1 = functools.reduce(jnp.add, [jnp.sum(p, axis=1, keepdims=True) for p in p1s])
    inv0 = 1.0 / l0
    inv1 = lam / l1
    o = None
    for (_, _, v, _), p0, p1 in zip(segs, p0s, p1s):
        part = _dot((p0 * inv0 - p1 * inv1).astype(BF16), v)
        o = part if o is None else o + part
    o = o * lax.rsqrt(jnp.mean(o * o, axis=-1, keepdims=True) + SUBLN_EPS) * subln_g
    return o * out_scale


def _tri(n):
    j = lax.broadcasted_iota(jnp.int32, (2 * n, n), 0)
    s = lax.broadcasted_iota(jnp.int32, (2 * n, n), 1)
    return (jnp.where(j >= n, j - n, j) >= s).astype(BF16)


def _stick_scores(q, blocks):
    scale = 1.0 / math.sqrt(D_HEAD)
    zs, splits = [], []
    for k, _, mask in blocks:
        z = _dot_nt(q, k) * scale
        sp = jnp.maximum(z, 0.0) + jnp.log(1.0 + jnp.exp(-jnp.abs(z)))
        if mask is not None:
            sp = jnp.where(mask, sp, 0.0)
        hi = sp.astype(BF16)
        lo = (sp - hi.astype(F32)).astype(BF16)
        zs.append(z)
        splits.append(jnp.concatenate([hi, lo], axis=1))
    return zs, splits


def _stick_output(zs, splits, blocks, tris):
    tq = zs[0].shape[0]
    incls = [None] * len(blocks)
    for n in sorted({z.shape[1] for z in zs}):
        idx = [i for i, z in enumerate(zs) if z.shape[1] == n]
        stacked = splits[idx[0]] if len(idx) == 1 else jnp.concatenate([splits[i] for i in idx], axis=0)
        inc = _dot(stacked, tris[n])
        for r, i in enumerate(idx):
            incls[i] = inc[r * tq:(r + 1) * tq, :]
    carry = None
    o = None
    for i in reversed(range(len(blocks))):
        _, v, mask = blocks[i]
        expo = zs[i] - incls[i]
        if carry is not None:
            expo = expo - carry
        a = jnp.exp(expo)
        if mask is not None:
            a = jnp.where(mask, a, 0.0)
        part = _dot(a.astype(BF16), v)
        o = part if o is None else o + part
        total = incls[i][:, 0:1]
        carry = total if carry is None else carry + total
    return o


def _attn_a_self_kernel(q_ref, k_ref, v_ref, lq1, lk1, lq2, lk2, g_ref, o_ref, kb, vb, *, tq, lam_init):
    s_len = q_ref.shape[0]
    kb[...] = k_ref[...].astype(BF16)
    vb[...] = v_ref[...].astype(BF16)
    lam = _lambda(lq1, lk1, lq2, lk2, lam_init)
    g = g_ref[...]
    masks = _Masks()
    nq = s_len // tq
    pending = None
    for qi in range(nq + 1):
        stage1 = None
        if qi < nq:
            r0 = qi * tq
            rows = slice(r0, r0 + tq)
            segs = []
            if qi > 0:
                segs.append((kb[0:r0, :D_HEAD], kb[0:r0, D_HEAD:], vb[0:r0, :], masks.chunk(r0, tq, 0, r0)))
            segs.append((kb[rows, :D_HEAD], kb[rows, D_HEAD:], vb[rows, :], masks.chunk(r0, tq, r0, tq)))
            stage1 = (rows, segs) + _diff_scores(q_ref[rows, :D_HEAD], q_ref[rows, D_HEAD:], segs)
        if pending is not None:
            prows, psegs, s0s, s1s = pending
            o_ref[prows, :] = _diff_output(s0s, s1s, psegs, lam, g, 1.0 - lam_init).astype(o_ref.dtype)
        pending = stage1


def _attn_a_self(q, k, v, lams, subln_g, b, s_len, lam_init):
    m, width = q.shape
    hw = 2 * D_HEAD
    tq = _tile(s_len, 256)
    assert tq % CHUNK == 0
    blk = pl.BlockSpec((s_len, hw), lambda i, h: (i, h))
    vec = pl.BlockSpec((1, D_HEAD), lambda i, h: (0, 0))
    return pl.pallas_call(
        functools.partial(_attn_a_self_kernel, tq=tq, lam_init=lam_init),
        grid=(b, width // hw),
        in_specs=[blk, blk, blk, vec, vec, vec, vec, pl.BlockSpec((1, hw), lambda i, h: (0, 0))],
        out_specs=blk,
        out_shape=jax.ShapeDtypeStruct((m, width), BF16),
        scratch_shapes=[pltpu.VMEM((s_len, hw), BF16), pltpu.VMEM((s_len, hw), BF16)],
        compiler_params=_params("arbitrary", "arbitrary"),
        name="diff_attn_self",
    )(q, k, v, *lams, subln_g)


def _attn_b_self_kernel(q_ref, k_ref, v_ref, o_ref, kb, vb, *, tq):
    s_len = q_ref.shape[0]
    kb[...] = k_ref[...].astype(BF16)
    vb[...] = v_ref[...].astype(BF16)
    tris = {tq: _tri(tq)}
    masks = _Masks()
    pending = None
    nq = s_len // tq
    work = [(hh, qi) for hh in range(q_ref.shape[1] // D_HEAD) for qi in range(nq)]
    for item in work + [None]:
        stage1 = None
        if item is not None:
            hh, qi = item
            cols = slice(hh * D_HEAD, (hh + 1) * D_HEAD)
            r0 = qi * tq
            rows = slice(r0, r0 + tq)
            blocks = []
            for j in range(qi + 1):
                keys = slice(j * tq, (j + 1) * tq)
                blocks.append((kb[keys, cols], vb[keys, cols], masks.before(r0, tq, j * tq, tq)))
            stage1 = (rows, cols, blocks) + _stick_scores(q_ref[rows, cols], blocks)
        if pending is not None:
            prows, pcols, pblocks, zs, splits = pending
            o_ref[prows, pcols] = _stick_output(zs, splits, pblocks, tris).astype(o_ref.dtype)
        pending = stage1


def _attn_b_self(q, k, v, b, s_len):
    m, width = q.shape
    tq = _tile(s_len, 256)
    hw = _tile(width, 2 * D_HEAD)
    blk = pl.BlockSpec((s_len, hw), lambda i, h: (i, h))
    return pl.pallas_call(
        functools.partial(_attn_b_self_kernel, tq=tq),
        grid=(b, width // hw),
        in_specs=[blk, blk, blk],
        out_specs=blk,
        out_shape=jax.ShapeDtypeStruct((m, width), BF16),
        scratch_shapes=[pltpu.VMEM((s_len, hw), BF16), pltpu.VMEM((s_len, hw), BF16)],
        compiler_params=_params("arbitrary", "arbitrary"),
        name="stick_attn_self",
    )(q, k, v)


def _attn_a_cached_kernel(q_ref, kc_ref, vc_ref, kn_ref, vn_ref, lq1, lk1, lq2, lk2, g_ref, o_ref, *, lam_init):
    t = q_ref.shape[0]
    p = kc_ref.shape[0]
    kc = kc_ref[...].astype(BF16)
    vc = vc_ref[...].astype(BF16)
    kn = kn_ref[...].astype(BF16)
    vn = vn_ref[...].astype(BF16)
    lam = _lambda(lq1, lk1, lq2, lk2, lam_init)
    masks = _Masks()
    segs = [(kc[:, :D_HEAD], kc[:, D_HEAD:], vc, masks.chunk(p, t, 0, p)),
            (kn[:, :D_HEAD], kn[:, D_HEAD:], vn, masks.chunk(p, t, p, t))]
    s0s, s1s = _diff_scores(q_ref[:, :D_HEAD], q_ref[:, D_HEAD:], segs)
    o_ref[...] = _diff_output(s0s, s1s, segs, lam, g_ref[...], 1.0 - lam_init).astype(o_ref.dtype)


def _attn_a_cached(q, kc, vc, kn, vn, lams, subln_g, b, t, lam_init):
    m, width = q.shape
    p = kc.shape[1]
    hw = 2 * D_HEAD
    blk = pl.BlockSpec((t, hw), lambda i, h: (i, h))
    cblk = pl.BlockSpec((None, p, hw), lambda i, h: (i, 0, h))
    vec = pl.BlockSpec((1, D_HEAD), lambda i, h: (0, 0))
    return pl.pallas_call(
        functools.partial(_attn_a_cached_kernel, lam_init=lam_init),
        grid=(b, width // hw),
        in_specs=[blk, cblk, cblk, blk, blk, vec, vec, vec, vec, pl.BlockSpec((1, hw), lambda i, h: (0, 0))],
        out_specs=blk,
        out_shape=jax.ShapeDtypeStruct((m, width), BF16),
        compiler_params=_params("arbitrary", "arbitrary"),
        name="diff_attn_cached",
    )(q, kc, vc, kn, vn, *lams, subln_g)


def _attn_b_cached_kernel(q_ref, kc_ref, vc_ref, kn_ref, vn_ref, o_ref, *, tk):
    t = q_ref.shape[0]
    p = kc_ref.shape[0]
    kc = kc_ref[...].astype(BF16)
    vc = vc_ref[...].astype(BF16)
    masks = _Masks()
    blocks = []
    for j in range(p // tk):
        keys = slice(j * tk, (j + 1) * tk)
        blocks.append((kc[keys, :], vc[keys, :], masks.before(p, t, j * tk, tk)))
    blocks.append((kn_ref[...].astype(BF16), vn_ref[...].astype(BF16), masks.before(p, t, p, t)))
    tris = {n: _tri(n) for n in {tk, t}}
    zs, splits = _stick_scores(q_ref[...], blocks)
    o_ref[...] = _stick_output(zs, splits, blocks, tris).astype(o_ref.dtype)


def _attn_b_cached(q, kc, vc, kn, vn, b, t):
    m, width = q.shape
    p = kc.shape[1]
    tk = _tile(p, 256)
    blk = pl.BlockSpec((t, D_HEAD), lambda i, h: (i, h))
    cblk = pl.BlockSpec((None, p, D_HEAD), lambda i, h: (i, 0, h))
    return pl.pallas_call(
        functools.partial(_attn_b_cached_kernel, tk=tk),
        grid=(b, width // D_HEAD),
        in_specs=[blk, cblk, cblk, blk, blk],
        out_specs=blk,
        out_shape=jax.ShapeDtypeStruct((m, width), BF16),
        compiler_params=_params("arbitrary", "arbitrary"),
        name="stick_attn_cached",
    )(q, kc, vc, kn, vn)


def _res_mm_kernel(*refs, nx):
    xs, ws = refs[:nx], refs[nx:2 * nx]
    xres_ref, gate_ref, o_ref = refs[2 * nx:]
    acc = _dot(xs[0][...], ws[0][...])
    for x_ref, w_ref in zip(xs[1:], ws[1:]):
        acc = acc + _dot(x_ref[...], w_ref[...])
    for s, rows in _stream_rows(gate_ref, o_ref.shape[0]):
        o_ref[rows, :] = xres_ref[rows, :] + gate_ref[s] * acc[rows, :]


def _res_matmul(xs, w, xres, gate, t, tm, tn_pref=512):
    m, n = xres.shape
    kp = xs[0].shape[1]
    nx = len(xs)
    assert all(x.shape == (m, kp) for x in xs) and w.shape == (nx * kp, n)
    tn = _tile(n, tn_pref)
    in_specs = [pl.BlockSpec((tm, kp), lambda i, j: (i, 0)) for _ in xs]
    in_specs += [pl.BlockSpec((kp, tn), functools.partial(lambda i, j, p: (p, j), p=p)) for p in range(nx)]
    in_specs += [pl.BlockSpec((tm, tn), lambda i, j: (i, j)), _stream_vec_spec(t, tm, tn, True)]
    return pl.pallas_call(
        functools.partial(_res_mm_kernel, nx=nx),
        grid=(m // tm, n // tn),
        in_specs=in_specs,
        out_specs=pl.BlockSpec((tm, tn), lambda i, j: (i, j)),
        out_shape=jax.ShapeDtypeStruct((m, n), F32),
        compiler_params=_params("arbitrary", "arbitrary"),
        name="res_matmul",
    )(*xs, *([w] * nx), xres, gate[:, None, :])


def _mlp_kernel(*refs, final, nchunk):
    if final:
        x_ref, ng_ref, sc_ref, sh_ref, wu_ref, wd_ref, g_ref, fg_ref, o_ref, h_scr = refs
    else:
        x_ref, ng_ref, sc_ref, sh_ref, wu_ref, wd_ref, g_ref, o_ref, h_scr = refs
    f = pl.program_id(1)
    streams = _stream_rows(g_ref, o_ref.shape[0])

    @pl.when(f == 0)
    def _():
        for s, rows in streams:
            h_scr[rows, :] = _norm_mod_rows(x_ref[rows, :], ng_ref[...], sc_ref[s], sh_ref[s])

    hid = _dot(h_scr[...], wu_ref[...])
    hid = jnp.square(jnp.maximum(hid, 0.0)).astype(BF16)
    d = o_ref.shape[1]
    cols = [slice(c * (d // nchunk), (c + 1) * (d // nchunk)) for c in range(nchunk)]

    @pl.when(f == 0)
    def _():
        for sl in cols:
            o_ref[:, sl] = _dot(hid, wd_ref[:, sl])

    @pl.when(f > 0)
    def _():
        for sl in cols:
            o_ref[:, sl] += _dot(hid, wd_ref[:, sl])

    @pl.when(f == pl.num_programs(1) - 1)
    def _():
        for s, rows in streams:
            x = x_ref[rows, :] + g_ref[s] * o_ref[rows, :]
            if final:
                x = x * lax.rsqrt(jnp.mean(x * x, axis=-1, keepdims=True) + NORM_EPS) * fg_ref[...]
            o_ref[rows, :] = x


def _mlp(x, norm_g, sc, sh, wu, wd, layer, gate, t, tm, final_g=None):
    m, d = x.shape
    dff = wd.shape[1]
    tf = _tile(dff, MLP_FF_TILE)
    vec = _stream_vec_spec(t, tm, d, False)
    in_specs = [pl.BlockSpec((tm, d), lambda i, f: (i, 0), pipeline_mode=pl.Buffered(1)),
                pl.BlockSpec((1, d), lambda i, f: (0, 0)), vec, vec,
                pl.BlockSpec((None, d, tf), lambda i, f: (layer, 0, f)),
                pl.BlockSpec((None, tf, d), lambda i, f: (layer, f, 0)),
                vec]
    args = [x, norm_g.reshape(1, d), sc[:, None, :], sh[:, None, :], wu, wd, gate[:, None, :]]
    if final_g is not None:
        in_specs.append(pl.BlockSpec((1, d), lambda i, f: (0, 0)))
        args.append(final_g.reshape(1, d))
    return pl.pallas_call(
        functools.partial(_mlp_kernel, final=final_g is not None, nchunk=max(1, d // 512)),
        grid=(m // tm, dff // tf),
        in_specs=in_specs,
        out_specs=pl.BlockSpec((tm, d), lambda i, f: (i, 0)),
        out_shape=jax.ShapeDtypeStruct((m, d), F32),
        scratch_shapes=[pltpu.VMEM((tm, d), BF16)],
        compiler_params=_params("arbitrary", "arbitrary"),
        name="mlp_final" if final_g is not None else "mlp",
    )(*args)


def _glu_kernel(x_ref, wa_ref, wb_ref, o_ref):
    x = x_ref[...]
    o_ref[...] = _dot(x, wa_ref[...]) * jax.nn.sigmoid(_dot(x, wb_ref[...]))


def _glu(x, w, tm):
    m, k = x.shape
    d = w.shape[1] // 2
    tn = _tile(d, 512)
    nj = d // tn
    return pl.pallas_call(
        _glu_kernel,
        grid=(m // tm, nj),
        in_specs=[pl.BlockSpec((tm, k), lambda i, j: (i, 0)),
                  pl.BlockSpec((k, tn), lambda i, j: (0, j)),
                  pl.BlockSpec((k, tn), lambda i, j: (0, j + nj))],
        out_specs=pl.BlockSpec((tm, tn), lambda i, j: (i, j)),
        out_shape=jax.ShapeDtypeStruct((m, d), F32),
        compiler_params=_params("arbitrary", "arbitrary"),
        name="pointwise_glu",
    )(x, w, w)


def _conv_kernel(u_ref, prev_ref, st_ref, dw_ref, dwb_ref, lng_ref, lnb_ref, y_ref, ns_ref, up, shifted, yacc,
                 *, ts, lane_chunk, row_chunk):
    ti = pl.program_id(1)
    d = u_ref.shape[1]
    off = CONV_HALO - (CONV_WIDTH - 1)

    @pl.when(ti == 0)
    def _():
        up[0:CONV_HALO, :] = st_ref[...]

    @pl.when(ti > 0)
    def _():
        up[0:CONV_HALO, :] = prev_ref[...]

    up[CONV_HALO:CONV_HALO + ts, :] = u_ref[...]

    def lane_body(c, carry):
        cols = pl.ds(pl.multiple_of(c * lane_chunk, lane_chunk), lane_chunk)
        dw = dw_ref[:, cols]
        bias = dwb_ref[:, cols]
        nrow = CONV_HALO + ts - SUBLANE
        for b in range(1, SUBLANE):
            shifted[b - 1, 0:nrow, :] = up[b:b + nrow, cols]
        for r0 in range(0, ts, row_chunk):
            acc = None
            for w in range(CONV_WIDTH):
                a, b = divmod(off + w, SUBLANE)
                r = r0 + a * SUBLANE
                src = up[r:r + row_chunk, cols] if b == 0 else shifted[b - 1, r:r + row_chunk, :]
                term = src * dw[w:w + 1, :]
                acc = term if acc is None else acc + term
            yacc[r0:r0 + row_chunk, cols] = acc + bias
        return carry

    lax.fori_loop(0, d // lane_chunk, lane_body, 0)
    ln_rows = _tile(ts, 32)
    for r0 in range(0, ts, ln_rows):
        y = yacc[r0:r0 + ln_rows, :]
        yc = y - jnp.mean(y, axis=-1, keepdims=True)
        var = jnp.mean(yc * yc, axis=-1, keepdims=True)
        yn = yc * lax.rsqrt(var + NORM_EPS) * lng_ref[...] + lnb_ref[...]
        y_ref[r0:r0 + ln_rows, :] = (yn * jax.nn.sigmoid(yn)).astype(y_ref.dtype)

    @pl.when(ti == pl.num_programs(1) - 1)
    def _():
        ns_ref[...] = up[CONV_HALO + ts - (CONV_WIDTH - 1):CONV_HALO + ts, :]


def _conv_ln_swish(u, state, dw, dw_b, ln_g, ln_b, b, s_len):
    m, d = u.shape
    ts = _tile(s_len, 256)
    assert ts % CONV_HALO == 0 and s_len >= CONV_WIDTH - 1
    nt = s_len // ts
    st = jnp.pad(state, ((0, 0), (CONV_HALO - (CONV_WIDTH - 1), 0), (0, 0)))
    hpb = ts // CONV_HALO
    vec = pl.BlockSpec((1, d), lambda i, t: (0, 0))
    lane_chunk = _tile(d, 256)
    y, ns = pl.pallas_call(
        functools.partial(_conv_kernel, ts=ts, lane_chunk=lane_chunk, row_chunk=_tile(ts, 128)),
        grid=(b, nt),
        in_specs=[pl.BlockSpec((ts, d), lambda i, t: (i * nt + t, 0)),
                  pl.BlockSpec((CONV_HALO, d), lambda i, t: (jnp.maximum((i * nt + t) * hpb - 1, 0), 0)),
                  pl.BlockSpec((None, CONV_HALO, d), lambda i, t: (i, 0, 0)),
                  pl.BlockSpec((CONV_WIDTH, d), lambda i, t: (0, 0)),
                  vec, vec, vec],
        out_specs=[pl.BlockSpec((ts, d), lambda i, t: (i * nt + t, 0)),
                   pl.BlockSpec((None, CONV_WIDTH - 1, d), lambda i, t: (i, 0, 0))],
        out_shape=[jax.ShapeDtypeStruct((m, d), BF16),
                   jax.ShapeDtypeStruct((b, CONV_WIDTH - 1, d), F32)],
        scratch_shapes=[pltpu.VMEM((CONV_HALO + ts, d), F32),
                        pltpu.VMEM((SUBLANE - 1, CONV_HALO + ts, lane_chunk), F32),
                        pltpu.VMEM((ts, d), F32)],
        compiler_params=_params("arbitrary", "arbitrary"),
        name="conv_ln_swish",
    )(u, u, st, dw, dw_b.reshape(1, d), ln_g.reshape(1, d), ln_b.reshape(1, d))
    return y, ns


def _rope_tables(pos):
    half = D_HEAD // 2
    inv_freq = jnp.power(ROPE_THETA, -jnp.arange(half, dtype=F32) / half)
    ang = pos.astype(F32)[:, None] * inv_freq[None, :]
    cos, sin = jnp.cos(ang), jnp.sin(ang)
    return jnp.concatenate([cos, cos], axis=-1), jnp.concatenate([-sin, sin], axis=-1)


def _trunk(x3, mod, cache, state, p, wb):
    b, t, d = x3.shape
    m = b * t
    mix_a = d // 2
    mix_b = d - mix_a
    x = x3.reshape(m, d)
    tm = _tile(m, 1024)
    tm_mlp = _tile(m, 512)
    tm_norm = _tile(m, 512) if t % _tile(m, 512) else _tile(t, 256)
    past_len = 0 if cache is None else cache[0].shape[1]
    rope_tabs = _rope_tables(past_len + jnp.arange(t))

    def split_mod(layer):
        return jnp.split(mod[layer], 6, axis=-1)

    sh_m, sc_m, g_m, sh_f, sc_f, g_f = split_mod(0)
    h = _norm_mod(x, p["norm_mix"][0], sc_m, sh_m, t, tm_norm)
    w_in = wb["w_attn_in"]
    qa = _proj(h, w_in, 0, mix_a, BF16, tm, rope_tabs, t)
    ka = _proj(h, w_in, mix_a, mix_a, F32, tm, rope_tabs, t)
    va = _proj(h, w_in, 2 * mix_a, mix_a, F32, tm)
    qb = _proj(h, w_in, 3 * mix_a, mix_b, BF16, tm)
    kb = _proj(h, w_in, 3 * mix_a + mix_b, mix_b, F32, tm)
    vb = _proj(h, w_in, 3 * mix_a + 2 * mix_b, mix_b, F32, tm)
    lams = [p[n][0].reshape(1, D_HEAD) for n in ("lambda_q1", "lambda_k1", "lambda_q2", "lambda_k2")]
    subln_g = p["diff_subln_g"][0].reshape(1, 2 * D_HEAD)
    lam_init = 0.8 - 0.6 * math.exp(-0.3 * 0)
    if cache is None:
        o_a = _attn_a_self(qa, ka, va, lams, subln_g, b, t, lam_init)
        o_b = _attn_b_self(qb, kb, vb, b, t)
    else:
        ck_a, cv_a, ck_b, cv_b = cache
        o_a = _attn_a_cached(qa, ck_a.reshape(b, past_len, mix_a), cv_a.reshape(b, past_len, mix_a),
                             ka, va, lams, subln_g, b, t, lam_init)
        o_b = _attn_b_cached(qb, ck_b.reshape(b, past_len, mix_b), cv_b.reshape(b, past_len, mix_b),
                             kb, vb, b, t)
    x = _res_matmul([o_a, o_b], wb["w_attn_out"], x, g_m, t, tm)
    x = _mlp(x, p["norm_mlp"][0], sc_f, sh_f, wb["mlp_up"], wb["mlp_down"], 0, g_f, t, tm_mlp)

    sh_m, sc_m, g_m, sh_f, sc_f, g_f = split_mod(1)
    h = _norm_mod(x, p["norm_mix"][1], sc_m, sh_m, t, tm_norm)
    u = _glu(h, wb["conv_pw1"], tm)
    if state is None:
        state = jnp.zeros((b, CONV_WIDTH - 1, d), F32)
    y, new_state = _conv_ln_swish(u, state, p["conv_dw"][0], p["conv_dw_b"][0], p["conv_ln_g"][0],
                                  p["conv_ln_b"][0], b, t)
    x = _res_matmul([y], wb["conv_pw2"], x, g_m, t, tm)
    y_out = _mlp(x, p["norm_mlp"][1], sc_f, sh_f, wb["mlp_up"], wb["mlp_down"], 1, g_f, t, tm_mlp,
                 final_g=p["final_g"])

    h_a, h_b = mix_a // (2 * D_HEAD), mix_b // D_HEAD
    return (y_out.reshape(b, t, d),
            ka.reshape(1, b, t, 2 * h_a, D_HEAD), va.reshape(1, b, t, h_a, 2 * D_HEAD),
            kb.reshape(1, b, t, h_b, D_HEAD), vb.reshape(1, b, t, h_b, D_HEAD),
            new_state[None])


def kernel(x_prompt, x_sample, c_prompt, c_sample, cache_k_diff, cache_v_diff, cache_k_sb, cache_v_sb, state_conv, w_mod, b_mod, norm_mix, norm_mlp, w_attn_in, w_attn_out, lambda_q1, lambda_k1, lambda_q2, lambda_k2, diff_subln_g, conv_pw1, conv_dw, conv_dw_b, conv_ln_g, conv_ln_b, conv_pw2, mlp_up, mlp_down, final_g):
    assert w_mod.shape[0] == 2 and w_attn_in.shape[0] == 1 and conv_pw1.shape[0] == 1
    p = dict(norm_mix=norm_mix, norm_mlp=norm_mlp, lambda_q1=lambda_q1, lambda_k1=lambda_k1,
             lambda_q2=lambda_q2, lambda_k2=lambda_k2, diff_subln_g=diff_subln_g, conv_dw=conv_dw,
             conv_dw_b=conv_dw_b, conv_ln_g=conv_ln_g, conv_ln_b=conv_ln_b, final_g=final_g)
    wb = dict(w_attn_in=w_attn_in[0].astype(BF16), w_attn_out=w_attn_out[0].astype(BF16),
              conv_pw1=conv_pw1[0].astype(BF16), conv_pw2=conv_pw2[0].astype(BF16),
              mlp_up=mlp_up.astype(BF16), mlp_down=mlp_down.astype(BF16))
    nb = c_prompt.shape[0]
    mod = _modulation(jnp.concatenate([c_prompt, c_sample], axis=0), w_mod, b_mod)
    out_p = _trunk(x_prompt, mod[:, :nb], None, None, p, wb)
    cache = (cache_k_diff[0], cache_v_diff[0], cache_k_sb[0], cache_v_sb[0])
    out_s = _trunk(x_sample, mod[:, nb:], cache, state_conv[0], p, wb)
    return (out_p[0], out_s[0]) + out_p[1:] + out_s[1:]
```

```python
import functools
import math

import jax
import jax.numpy as jnp
import numpy as np
from jax import lax
from jax.experimental import pallas as pl
from jax.experimental.pallas import tpu as pltpu

CHUNK = 64
D_HEAD = 128
CONV_WIDTH = 31
ROPE_THETA = 10000.0
NORM_EPS = 1e-6
SUBLN_EPS = 1e-5
NEG_INF = -1e30

F32 = jnp.float32
BF16 = jnp.bfloat16

V7X_VMEM_LIMIT_BYTES = 60000 * 1024
SUBLANE = 8
MLP_FF_TILE = 512
CONV_HALO = -(-(CONV_WIDTH - 1) // SUBLANE) * SUBLANE


def _params(*sem):
    return pltpu.CompilerParams(dimension_semantics=sem, vmem_limit_bytes=V7X_VMEM_LIMIT_BYTES)


def _tile(n, pref):
    t = min(n, pref)
    while n % t:
        t -= 1
    return t


def _dot(a, b):
    return jnp.dot(a, b, preferred_element_type=F32)


def _dot_nt(a, b):
    return lax.dot_general(a, b, (((1,), (1,)), ((), ())), preferred_element_type=F32)


def _stream_vec_spec(t, tm, width, with_col):
    if t % tm == 0:
        spt, tps = 1, t // tm
    else:
        assert tm % t == 0
        spt, tps = tm // t, 1
    if with_col:
        return pl.BlockSpec((spt, 1, width), lambda i, j: (i // tps, 0, j))
    return pl.BlockSpec((spt, 1, width), lambda i, *_: (i // tps, 0, 0))


def _stream_rows(vec_ref, tm):
    spt = vec_ref.shape[0]
    rows = tm // spt
    return [(s, slice(s * rows, (s + 1) * rows)) for s in range(spt)]


def _mod_kernel(c_ref, w_ref, b_ref, o_ref):
    c = c_ref[...]
    cs = (c * jax.nn.sigmoid(c)).astype(BF16)
    o_ref[...] = _dot(cs, w_ref[...].astype(BF16)) + b_ref[...]


def _modulation(c_all, w_mod, b_mod):
    nl, d, n = w_mod.shape
    r = c_all.shape[0]
    tn = _tile(n, 512)
    return pl.pallas_call(
        _mod_kernel,
        grid=(nl, n // tn),
        in_specs=[pl.BlockSpec((r, d), lambda l, j: (0, 0)),
                  pl.BlockSpec((None, d, tn), lambda l, j: (l, 0, j)),
                  pl.BlockSpec((None, 1, tn), lambda l, j: (l, 0, j))],
        out_specs=pl.BlockSpec((None, r, tn), lambda l, j: (l, 0, j)),
        out_shape=jax.ShapeDtypeStruct((nl, r, n), F32),
        compiler_params=_params("arbitrary", "arbitrary"),
        name="modulation",
    )(c_all, w_mod, b_mod.reshape(nl, 1, n))


def _norm_mod_rows(x, g, sc, sh):
    y = x * lax.rsqrt(jnp.mean(x * x, axis=-1, keepdims=True) + NORM_EPS) * g
    return (y * (1.0 + sc) + sh).astype(BF16)


def _norm_mod_kernel(x_ref, g_ref, sc_ref, sh_ref, o_ref):
    for s, rows in _stream_rows(sc_ref, x_ref.shape[0]):
        o_ref[rows, :] = _norm_mod_rows(x_ref[rows, :], g_ref[...], sc_ref[s], sh_ref[s])


def _norm_mod(x, g, sc, sh, t, tm):
    m, d = x.shape
    vec = _stream_vec_spec(t, tm, d, False)
    return pl.pallas_call(
        _norm_mod_kernel,
        grid=(m // tm,),
        in_specs=[pl.BlockSpec((tm, d), lambda i: (i, 0)),
                  pl.BlockSpec((1, d), lambda i: (0, 0)), vec, vec],
        out_specs=pl.BlockSpec((tm, d), lambda i: (i, 0)),
        out_shape=jax.ShapeDtypeStruct((m, d), BF16),
        compiler_params=_params("arbitrary"),
        name="norm_mod",
    )(x, g.reshape(1, d), sc[:, None, :], sh[:, None, :])


def _proj_kernel(*refs, rope):
    if rope:
        x_ref, w_ref, cos_ref, sin_ref, o_ref = refs
    else:
        x_ref, w_ref, o_ref = refs
    acc = _dot(x_ref[...], w_ref[...])
    if rope:
        cos, sin = cos_ref[...], sin_ref[...]
        for h in range(acc.shape[1] // D_HEAD):
            sl = slice(h * D_HEAD, (h + 1) * D_HEAD)
            xh = acc[:, sl]
            o_ref[:, sl] = (xh * cos + pltpu.roll(xh, D_HEAD // 2, 1) * sin).astype(o_ref.dtype)
    else:
        o_ref[...] = acc.astype(o_ref.dtype)


def _proj(x, w, col_off, n, out_dtype, tm, rope_tabs=None, t=None):
    m, k = x.shape
    tn = _tile(n, 1024)
    assert col_off % tn == 0
    joff = col_off // tn

    def spec(shape, fn):
        return pl.BlockSpec(shape, lambda j, i: fn(i, j))

    in_specs = [spec((tm, k), lambda i, j: (i, 0)), spec((k, tn), lambda i, j: (0, j + joff))]
    args = [x, w]
    if rope_tabs is not None:
        cos, sin = rope_tabs
        if t % tm == 0:
            nblk = t // tm
            tab_map = lambda i, j: (i % nblk, 0)
        else:
            assert tm % t == 0
            cos, sin = jnp.tile(cos, (tm // t, 1)), jnp.tile(sin, (tm // t, 1))
            tab_map = lambda i, j: (0, 0)
        in_specs += [spec((tm, D_HEAD), tab_map), spec((tm, D_HEAD), tab_map)]
        args += [cos, sin]
    return pl.pallas_call(
        functools.partial(_proj_kernel, rope=rope_tabs is not None),
        grid=(n // tn, m // tm),
        in_specs=in_specs,
        out_specs=spec((tm, tn), lambda i, j: (i, j)),
        out_shape=jax.ShapeDtypeStruct((m, n), out_dtype),
        compiler_params=_params("arbitrary", "arbitrary"),
        name="proj_rope" if rope_tabs is not None else "proj",
    )(*args)


class _Masks:
    def __init__(self):
        self._cache = {}

    def _get(self, kind, q0, tq, k0, n):
        qp = np.arange(q0, q0 + tq)[:, None]
        kp = np.arange(k0, k0 + n)[None, :]
        vis = (kp // CHUNK <= qp // CHUNK) if kind == "chunk" else (kp < qp)
        if vis.all():
            return None
        assert vis.any()
        aligned = q0 % CHUNK == 0 and k0 % CHUNK == 0
        key = (kind, q0 - k0, tq, n) if (aligned or kind == "before") else (kind, q0, k0, tq, n)
        if key not in self._cache:
            row = lax.broadcasted_iota(jnp.int32, (tq, n), 0) + q0
            col = lax.broadcasted_iota(jnp.int32, (tq, n), 1) + k0
            self._cache[key] = (col // CHUNK <= row // CHUNK) if kind == "chunk" else (col < row)
        return self._cache[key]

    def chunk(self, q0, tq, k0, n):
        return self._get("chunk", q0, tq, k0, n)

    def before(self, q0, tq, k0, n):
        return self._get("before", q0, tq, k0, n)


def _lambda(lq1, lk1, lq2, lk2, lam_init):
    return (jnp.exp(jnp.sum(lq1[...] * lk1[...], keepdims=True))
            - jnp.exp(jnp.sum(lq2[...] * lk2[...], keepdims=True)) + lam_init)


def _diff_scores(q0, q1, segs):
    scale = 1.0 / math.sqrt(D_HEAD)
    s0s, s1s = [], []
    for k0, k1, _, mask in segs:
        s0 = _dot_nt(q0, k0) * scale
        s1 = _dot_nt(q1, k1) * scale
        if mask is not None:
            s0 = jnp.where(mask, s0, NEG_INF)
            s1 = jnp.where(mask, s1, NEG_INF)
        s0s.append(s0)
        s1s.append(s1)
    return s0s, s1s


def _diff_output(s0s, s1s, segs, lam, subln_g, out_scale):
    m0 = functools.reduce(jnp.maximum, [jnp.max(s, axis=1, keepdims=True) for s in s0s])
    m1 = functools.reduce(jnp.maximum, [jnp.max(s, axis=1, keepdims=True) for s in s1s])
    p0s = [jnp.exp(s - m0) for s in s0s]
    p1s = [jnp.exp(s - m1) for s in s1s]
    l0 = functools.reduce(jnp.add, [jnp.sum(p, axis=1, keepdims=True) for p in p0s])
    l1 = functools.reduce(jnp.add, [jnp.sum(p, axis=1, keepdims=True) for p in p1s])
    inv0 = 1.0 / l0
    inv1 = lam / l1
    o = None
    for (_, _, v, _), p0, p1 in zip(segs, p0s, p1s):
        part = _dot((p0 * inv0 - p1 * inv1).astype(BF16), v)
        o = part if o is None else o + part
    o = o * lax.rsqrt(jnp.mean(o * o, axis=-1, keepdims=True) + SUBLN_EPS) * subln_g
    return o * out_scale


def _tri(n):
    j = lax.broadcasted_iota(jnp.int32, (2 * n, n), 0)
    s = lax.broadcasted_iota(jnp.int32, (2 * n, n), 1)
    return (jnp.where(j >= n, j - n, j) >= s).astype(BF16)


def _stick_scores(q, blocks):
    scale = 1.0 / math.sqrt(D_HEAD)
    zs, splits = [], []
    for k, _, mask in blocks:
        z = _dot_nt(q, k) * scale
        sp = jnp.maximum(z, 0.0) + jnp.log(1.0 + jnp.exp(-jnp.abs(z)))
        if mask is not None:
            sp = jnp.where(mask, sp, 0.0)
        hi = sp.astype(BF16)
        lo = (sp - hi.astype(F32)).astype(BF16)
        zs.append(z)
        splits.append(jnp.concatenate([hi, lo], axis=1))
    return zs, splits


def _stick_cumsums(zs, splits, tris):
    tq = zs[0].shape[0]
    incls = [None] * len(zs)
    for n in sorted({z.shape[1] for z in zs}):
        idx = [i for i, z in enumerate(zs) if z.shape[1] == n]
        stacked = splits[idx[0]] if len(idx) == 1 else jnp.concatenate([splits[i] for i in idx], axis=0)
        inc = _dot(stacked, tris[n])
        for r, i in enumerate(idx):
            incls[i] = inc[r * tq:(r + 1) * tq, :]
    return incls


def _stick_output(zs, incls, blocks):
    carry = None
    o = None
    for i in reversed(range(len(blocks))):
        _, v, mask = blocks[i]
        expo = zs[i] - incls[i]
        if carry is not None:
            expo = expo - carry
        a = jnp.exp(expo)
        if mask is not None:
            a = jnp.where(mask, a, 0.0)
        part = _dot(a.astype(BF16), v)
        o = part if o is None else o + part
        total = incls[i][:, 0:1]
        carry = total if carry is None else carry + total
    return o


def _attn_a_self_kernel(q_ref, k_ref, v_ref, lq1, lk1, lq2, lk2, g_ref, o_ref, kb, vb, *, tq, lam_init):
    s_len = q_ref.shape[0]
    kb[...] = k_ref[...].astype(BF16)
    vb[...] = v_ref[...].astype(BF16)
    lam = _lambda(lq1, lk1, lq2, lk2, lam_init)
    g = g_ref[...]
    masks = _Masks()
    nq = s_len // tq
    pending = None
    for qi in range(nq + 1):
        stage1 = None
        if qi < nq:
            r0 = qi * tq
            rows = slice(r0, r0 + tq)
            segs = []
            if qi > 0:
                segs.append((kb[0:r0, :D_HEAD], kb[0:r0, D_HEAD:], vb[0:r0, :], masks.chunk(r0, tq, 0, r0)))
            segs.append((kb[rows, :D_HEAD], kb[rows, D_HEAD:], vb[rows, :], masks.chunk(r0, tq, r0, tq)))
            stage1 = (rows, segs) + _diff_scores(q_ref[rows, :D_HEAD], q_ref[rows, D_HEAD:], segs)
        if pending is not None:
            prows, psegs, s0s, s1s = pending
            o_ref[prows, :] = _diff_output(s0s, s1s, psegs, lam, g, 1.0 - lam_init).astype(o_ref.dtype)
        pending = stage1


def _attn_a_self(q, k, v, lams, subln_g, b, s_len, lam_init):
    m, width = q.shape
    hw = 2 * D_HEAD
    tq = _tile(s_len, 256)
    assert tq % CHUNK == 0
    blk = pl.BlockSpec((s_len, hw), lambda i, h: (i, h))
    vec = pl.BlockSpec((1, D_HEAD), lambda i, h: (0, 0))
    return pl.pallas_call(
        functools.partial(_attn_a_self_kernel, tq=tq, lam_init=lam_init),
        grid=(b, width // hw),
        in_specs=[blk, blk, blk, vec, vec, vec, vec, pl.BlockSpec((1, hw), lambda i, h: (0, 0))],
        out_specs=blk,
        out_shape=jax.ShapeDtypeStruct((m, width), BF16),
        scratch_shapes=[pltpu.VMEM((s_len, hw), BF16), pltpu.VMEM((s_len, hw), BF16)],
        compiler_params=_params("arbitrary", "arbitrary"),
        name="diff_attn_self",
    )(q, k, v, *lams, subln_g)


def _attn_b_self_kernel(q_ref, k_ref, v_ref, o_ref, kb, vb, *, tq):
    s_len = q_ref.shape[0]
    kb[...] = k_ref[...].astype(BF16)
    vb[...] = v_ref[...].astype(BF16)
    tris = {tq: _tri(tq)}
    masks = _Masks()
    pending = None
    nq = s_len // tq
    work = [(hh, qi) for hh in range(q_ref.shape[1] // D_HEAD) for qi in range(nq)]
    for item in work + [None]:
        stage1 = None
        if item is not None:
            hh, qi = item
            cols = slice(hh * D_HEAD, (hh + 1) * D_HEAD)
            r0 = qi * tq
            rows = slice(r0, r0 + tq)
            blocks = []
            for j in range(qi + 1):
                keys = slice(j * tq, (j + 1) * tq)
                blocks.append((kb[keys, cols], vb[keys, cols], masks.before(r0, tq, j * tq, tq)))
            zs, splits = _stick_scores(q_ref[rows, cols], blocks)
            stage1 = (rows, cols, blocks, zs, _stick_cumsums(zs, splits, tris))
        if pending is not None:
            prows, pcols, pblocks, zs, incls = pending
            o_ref[prows, pcols] = _stick_output(zs, incls, pblocks).astype(o_ref.dtype)
        pending = stage1


def _attn_b_self(q, k, v, b, s_len):
    m, width = q.shape
    tq = _tile(s_len, 256)
    hw = _tile(width, 2 * D_HEAD)
    blk = pl.BlockSpec((s_len, hw), lambda i, h: (i, h))
    return pl.pallas_call(
        functools.partial(_attn_b_self_kernel, tq=tq),
        grid=(b, width // hw),
        in_specs=[blk, blk, blk],
        out_specs=blk,
        out_shape=jax.ShapeDtypeStruct((m, width), BF16),
        scratch_shapes=[pltpu.VMEM((s_len, hw), BF16), pltpu.VMEM((s_len, hw), BF16)],
        compiler_params=_params("arbitrary", "arbitrary"),
        name="stick_attn_self",
    )(q, k, v)


def _attn_a_cached_kernel(q_ref, kc_ref, vc_ref, kn_ref, vn_ref, lq1, lk1, lq2, lk2, g_ref, o_ref, *, lam_init):
    t = q_ref.shape[0]
    p = kc_ref.shape[0]
    kc = kc_ref[...].astype(BF16)
    vc = vc_ref[...].astype(BF16)
    kn = kn_ref[...].astype(BF16)
    vn = vn_ref[...].astype(BF16)
    lam = _lambda(lq1, lk1, lq2, lk2, lam_init)
    masks = _Masks()
    segs = [(kc[:, :D_HEAD], kc[:, D_HEAD:], vc, masks.chunk(p, t, 0, p)),
            (kn[:, :D_HEAD], kn[:, D_HEAD:], vn, masks.chunk(p, t, p, t))]
    s0s, s1s = _diff_scores(q_ref[:, :D_HEAD], q_ref[:, D_HEAD:], segs)
    o_ref[...] = _diff_output(s0s, s1s, segs, lam, g_ref[...], 1.0 - lam_init).astype(o_ref.dtype)


def _attn_a_cached(q, kc, vc, kn, vn, lams, subln_g, b, t, lam_init):
    m, width = q.shape
    p = kc.shape[1]
    hw = 2 * D_HEAD
    blk = pl.BlockSpec((t, hw), lambda i, h: (i, h))
    cblk = pl.BlockSpec((None, p, hw), lambda i, h: (i, 0, h))
    vec = pl.BlockSpec((1, D_HEAD), lambda i, h: (0, 0))
    return pl.pallas_call(
        functools.partial(_attn_a_cached_kernel, lam_init=lam_init),
        grid=(b, width // hw),
        in_specs=[blk, cblk, cblk, blk, blk, vec, vec, vec, vec, pl.BlockSpec((1, hw), lambda i, h: (0, 0))],
        out_specs=blk,
        out_shape=jax.ShapeDtypeStruct((m, width), BF16),
        compiler_params=_params("arbitrary", "arbitrary"),
        name="diff_attn_cached",
    )(q, kc, vc, kn, vn, *lams, subln_g)


def _attn_b_cached_kernel(q_ref, kc_ref, vc_ref, kn_ref, vn_ref, o_ref, *, tk):
    t = q_ref.shape[0]
    p = kc_ref.shape[0]
    kc = kc_ref[...].astype(BF16)
    vc = vc_ref[...].astype(BF16)
    masks = _Masks()
    blocks = []
    for j in range(p // tk):
        keys = slice(j * tk, (j + 1) * tk)
        blocks.append((kc[keys, :], vc[keys, :], masks.before(p, t, j * tk, tk)))
    blocks.append((kn_ref[...].astype(BF16), vn_ref[...].astype(BF16), masks.before(p, t, p, t)))
    tris = {n: _tri(n) for n in {tk, t}}
    zs, splits = _stick_scores(q_ref[...], blocks)
    o_ref[...] = _stick_output(zs, _stick_cumsums(zs, splits, tris), blocks).astype(o_ref.dtype)


def _attn_b_cached(q, kc, vc, kn, vn, b, t):
    m, width = q.shape
    p = kc.shape[1]
    tk = _tile(p, 256)
    blk = pl.BlockSpec((t, D_HEAD), lambda i, h: (i, h))
    cblk = pl.BlockSpec((None, p, D_HEAD), lambda i, h: (i, 0, h))
    return pl.pallas_call(
        functools.partial(_attn_b_cached_kernel, tk=tk),
        grid=(b, width // D_HEAD),
        in_specs=[blk, cblk, cblk, blk, blk],
        out_specs=blk,
        out_shape=jax.ShapeDtypeStruct((m, width), BF16),
        compiler_params=_params("arbitrary", "arbitrary"),
        name="stick_attn_cached",
    )(q, kc, vc, kn, vn)


def _res_mm_kernel(*refs, nx):
    xs, ws = refs[:nx], refs[nx:2 * nx]
    xres_ref, gate_ref, o_ref = refs[2 * nx:]
    acc = _dot(xs[0][...], ws[0][...])
    for x_ref, w_ref in zip(xs[1:], ws[1:]):
        acc = acc + _dot(x_ref[...], w_ref[...])
    for s, rows in _stream_rows(gate_ref, o_ref.shape[0]):
        o_ref[rows, :] = xres_ref[rows, :] + gate_ref[s] * acc[rows, :]


def _res_matmul(xs, w, xres, gate, t, tm, tn_pref=512):
    m, n = xres.shape
    kp = xs[0].shape[1]
    nx = len(xs)
    assert all(x.shape == (m, kp) for x in xs) and w.shape == (nx * kp, n)
    tn = _tile(n, tn_pref)
    in_specs = [pl.BlockSpec((tm, kp), lambda i, j: (i, 0)) for _ in xs]
    in_specs += [pl.BlockSpec((kp, tn), functools.partial(lambda i, j, p: (p, j), p=p)) for p in range(nx)]
    in_specs += [pl.BlockSpec((tm, tn), lambda i, j: (i, j)), _stream_vec_spec(t, tm, tn, True)]
    return pl.pallas_call(
        functools.partial(_res_mm_kernel, nx=nx),
        grid=(m // tm, n // tn),
        in_specs=in_specs,
        out_specs=pl.BlockSpec((tm, tn), lambda i, j: (i, j)),
        out_shape=jax.ShapeDtypeStruct((m, n), F32),
        compiler_params=_params("arbitrary", "arbitrary"),
        name="res_matmul",
    )(*xs, *([w] * nx), xres, gate[:, None, :])


def _mlp_kernel(*refs, final, nchunk):
    if final:
        x_ref, ng_ref, sc_ref, sh_ref, wu_ref, wd_ref, g_ref, fg_ref, o_ref, h_scr = refs
    else:
        x_ref, ng_ref, sc_ref, sh_ref, wu_ref, wd_ref, g_ref, o_ref, h_scr = refs
    f = pl.program_id(1)
    streams = _stream_rows(g_ref, o_ref.shape[0])

    @pl.when(f == 0)
    def _():
        for s, rows in streams:
            h_scr[rows, :] = _norm_mod_rows(x_ref[rows, :], ng_ref[...], sc_ref[s], sh_ref[s])

    hid = _dot(h_scr[...], wu_ref[...])
    hid = jnp.square(jnp.maximum(hid, 0.0)).astype(BF16)
    d = o_ref.shape[1]
    cols = [slice(c * (d // nchunk), (c + 1) * (d // nchunk)) for c in range(nchunk)]

    @pl.when(f == 0)
    def _():
        for sl in cols:
            o_ref[:, sl] = _dot(hid, wd_ref[:, sl])

    @pl.when(f > 0)
    def _():
        for sl in cols:
            o_ref[:, sl] += _dot(hid, wd_ref[:, sl])

    @pl.when(f == pl.num_programs(1) - 1)
    def _():
        for s, rows in streams:
            x = x_ref[rows, :] + g_ref[s] * o_ref[rows, :]
            if final:
                x = x * lax.rsqrt(jnp.mean(x * x, axis=-1, keepdims=True) + NORM_EPS) * fg_ref[...]
            o_ref[rows, :] = x


def _mlp(x, norm_g, sc, sh, wu, wd, layer, gate, t, tm, final_g=None):
    m, d = x.shape
    dff = wd.shape[1]
    tf = _tile(dff, MLP_FF_TILE)
    vec = _stream_vec_spec(t, tm, d, False)
    in_specs = [pl.BlockSpec((tm, d), lambda i, f: (i, 0), pipeline_mode=pl.Buffered(1)),
                pl.BlockSpec((1, d), lambda i, f: (0, 0)), vec, vec,
                pl.BlockSpec((None, d, tf), lambda i, f: (layer, 0, f)),
                pl.BlockSpec((None, tf, d), lambda i, f: (layer, f, 0)),
                vec]
    args = [x, norm_g.reshape(1, d), sc[:, None, :], sh[:, None, :], wu, wd, gate[:, None, :]]
    if final_g is not None:
        in_specs.append(pl.BlockSpec((1, d), lambda i, f: (0, 0)))
        args.append(final_g.reshape(1, d))
    return pl.pallas_call(
        functools.partial(_mlp_kernel, final=final_g is not None, nchunk=max(1, d // 512)),
        grid=(m // tm, dff // tf),
        in_specs=in_specs,
        out_specs=pl.BlockSpec((tm, d), lambda i, f: (i, 0)),
        out_shape=jax.ShapeDtypeStruct((m, d), F32),
        scratch_shapes=[pltpu.VMEM((tm, d), BF16)],
        compiler_params=_params("arbitrary", "arbitrary"),
        name="mlp_final" if final_g is not None else "mlp",
    )(*args)


def _glu_kernel(x_ref, wa_ref, wb_ref, o_ref):
    x = x_ref[...]
    o_ref[...] = _dot(x, wa_ref[...]) * jax.nn.sigmoid(_dot(x, wb_ref[...]))


def _glu(x, w, tm):
    m, k = x.shape
    d = w.shape[1] // 2
    tn = _tile(d, 512)
    nj = d // tn
    return pl.pallas_call(
        _glu_kernel,
        grid=(m // tm, nj),
        in_specs=[pl.BlockSpec((tm, k), lambda i, j: (i, 0)),
                  pl.BlockSpec((k, tn), lambda i, j: (0, j)),
                  pl.BlockSpec((k, tn), lambda i, j: (0, j + nj))],
        out_specs=pl.BlockSpec((tm, tn), lambda i, j: (i, j)),
        out_shape=jax.ShapeDtypeStruct((m, d), F32),
        compiler_params=_params("arbitrary", "arbitrary"),
        name="pointwise_glu",
    )(x, w, w)


def _conv_kernel(u_ref, prev_ref, st_ref, dw_ref, dwb_ref, lng_ref, lnb_ref, y_ref, ns_ref, up, shifted, yacc,
                 *, ts, lane_chunk, row_chunk):
    ti = pl.program_id(1)
    d = u_ref.shape[1]
    off = CONV_HALO - (CONV_WIDTH - 1)

    @pl.when(ti == 0)
    def _():
        up[0:CONV_HALO, :] = st_ref[...]

    @pl.when(ti > 0)
    def _():
        up[0:CONV_HALO, :] = prev_ref[...]

    up[CONV_HALO:CONV_HALO + ts, :] = u_ref[...]

    def lane_body(c, carry):
        cols = pl.ds(pl.multiple_of(c * lane_chunk, lane_chunk), lane_chunk)
        dw = dw_ref[:, cols]
        bias = dwb_ref[:, cols]
        nrow = CONV_HALO + ts - SUBLANE
        for b in range(1, SUBLANE):
            shifted[b - 1, 0:nrow, :] = up[b:b + nrow, cols]
        for r0 in range(0, ts, row_chunk):
            acc = None
            for w in range(CONV_WIDTH):
                a, b = divmod(off + w, SUBLANE)
                r = r0 + a * SUBLANE
                src = up[r:r + row_chunk, cols] if b == 0 else shifted[b - 1, r:r + row_chunk, :]
                term = src * dw[w:w + 1, :]
                acc = term if acc is None else acc + term
            yacc[r0:r0 + row_chunk, cols] = acc + bias
        return carry

    lax.fori_loop(0, d // lane_chunk, lane_body, 0)
    ln_rows = _tile(ts, 32)
    for r0 in range(0, ts, ln_rows):
        y = yacc[r0:r0 + ln_rows, :]
        yc = y - jnp.mean(y, axis=-1, keepdims=True)
        var = jnp.mean(yc * yc, axis=-1, keepdims=True)
        yn = yc * lax.rsqrt(var + NORM_EPS) * lng_ref[...] + lnb_ref[...]
        y_ref[r0:r0 + ln_rows, :] = (yn * jax.nn.sigmoid(yn)).astype(y_ref.dtype)

    @pl.when(ti == pl.num_programs(1) - 1)
    def _():
        ns_ref[...] = up[CONV_HALO + ts - (CONV_WIDTH - 1):CONV_HALO + ts, :]


def _conv_ln_swish(u, state, dw, dw_b, ln_g, ln_b, b, s_len):
    m, d = u.shape
    ts = _tile(s_len, 256)
    assert ts % CONV_HALO == 0 and s_len >= CONV_WIDTH - 1
    nt = s_len // ts
    st = jnp.pad(state, ((0, 0), (CONV_HALO - (CONV_WIDTH - 1), 0), (0, 0)))
    hpb = ts // CONV_HALO
    vec = pl.BlockSpec((1, d), lambda i, t: (0, 0))
    lane_chunk = _tile(d, 256)
    y, ns = pl.pallas_call(
        functools.partial(_conv_kernel, ts=ts, lane_chunk=lane_chunk, row_chunk=_tile(ts, 128)),
        grid=(b, nt),
        in_specs=[pl.BlockSpec((ts, d), lambda i, t: (i * nt + t, 0)),
                  pl.BlockSpec((CONV_HALO, d), lambda i, t: (jnp.maximum((i * nt + t) * hpb - 1, 0), 0)),
                  pl.BlockSpec((None, CONV_HALO, d), lambda i, t: (i, 0, 0)),
                  pl.BlockSpec((CONV_WIDTH, d), lambda i, t: (0, 0)),
                  vec, vec, vec],
        out_specs=[pl.BlockSpec((ts, d), lambda i, t: (i * nt + t, 0)),
                   pl.BlockSpec((None, CONV_WIDTH - 1, d), lambda i, t: (i, 0, 0))],
        out_shape=[jax.ShapeDtypeStruct((m, d), BF16),
                   jax.ShapeDtypeStruct((b, CONV_WIDTH - 1, d), F32)],
        scratch_shapes=[pltpu.VMEM((CONV_HALO + ts, d), F32),
                        pltpu.VMEM((SUBLANE - 1, CONV_HALO + ts, lane_chunk), F32),
                        pltpu.VMEM((ts, d), F32)],
        compiler_params=_params("arbitrary", "arbitrary"),
        name="conv_ln_swish",
    )(u, u, st, dw, dw_b.reshape(1, d), ln_g.reshape(1, d), ln_b.reshape(1, d))
    return y, ns


def _rope_tables(pos):
    half = D_HEAD // 2
    inv_freq = jnp.power(ROPE_THETA, -jnp.arange(half, dtype=F32) / half)
    ang = pos.astype(F32)[:, None] * inv_freq[None, :]
    cos, sin = jnp.cos(ang), jnp.sin(ang)
    return jnp.concatenate([cos, cos], axis=-1), jnp.concatenate([-sin, sin], axis=-1)


def _trunk(x3, mod, cache, state, p, wb):
    b, t, d = x3.shape
    m = b * t
    mix_a = d // 2
    mix_b = d - mix_a
    x = x3.reshape(m, d)
    tm = _tile(m, 1024)
    tm_mlp = _tile(m, 512)
    tm_norm = _tile(m, 512) if t % _tile(m, 512) else _tile(t, 256)
    past_len = 0 if cache is None else cache[0].shape[1]
    rope_tabs = _rope_tables(past_len + jnp.arange(t))

    def split_mod(layer):
        return jnp.split(mod[layer], 6, axis=-1)

    sh_m, sc_m, g_m, sh_f, sc_f, g_f = split_mod(0)
    h = _norm_mod(x, p["norm_mix"][0], sc_m, sh_m, t, tm_norm)
    w_in = wb["w_attn_in"]
    qa = _proj(h, w_in, 0, mix_a, BF16, tm, rope_tabs, t)
    ka = _proj(h, w_in, mix_a, mix_a, F32, tm, rope_tabs, t)
    va = _proj(h, w_in, 2 * mix_a, mix_a, F32, tm)
    qb = _proj(h, w_in, 3 * mix_a, mix_b, BF16, tm)
    kb = _proj(h, w_in, 3 * mix_a + mix_b, mix_b, F32, tm)
    vb = _proj(h, w_in, 3 * mix_a + 2 * mix_b, mix_b, F32, tm)
    lams = [p[n][0].reshape(1, D_HEAD) for n in ("lambda_q1", "lambda_k1", "lambda_q2", "lambda_k2")]
    subln_g = p["diff_subln_g"][0].reshape(1, 2 * D_HEAD)
    lam_init = 0.8 - 0.6 * math.exp(-0.3 * 0)
    if cache is None:
        o_a = _attn_a_self(qa, ka, va, lams, subln_g, b, t, lam_init)
        o_b = _attn_b_self(qb, kb, vb, b, t)
    else:
        ck_a, cv_a, ck_b, cv_b = cache
        o_a = _attn_a_cached(qa, ck_a.reshape(b, past_len, mix_a), cv_a.reshape(b, past_len, mix_a),
                             ka, va, lams, subln_g, b, t, lam_init)
        o_b = _attn_b_cached(qb, ck_b.reshape(b, past_len, mix_b), cv_b.reshape(b, past_len, mix_b),
                             kb, vb, b, t)
    x = _res_matmul([o_a, o_b], wb["w_attn_out"], x, g_m, t, tm)
    x = _mlp(x, p["norm_mlp"][0], sc_f, sh_f, wb["mlp_up"], wb["mlp_down"], 0, g_f, t, tm_mlp)

    sh_m, sc_m, g_m, sh_f, sc_f, g_f = split_mod(1)
    h = _norm_mod(x, p["norm_mix"][1], sc_m, sh_m, t, tm_norm)
    u = _glu(h, wb["conv_pw1"], tm)
    if state is None:
        state = jnp.zeros((b, CONV_WIDTH - 1, d), F32)
    y, new_state = _conv_ln_swish(u, state, p["conv_dw"][0], p["conv_dw_b"][0], p["conv_ln_g"][0],
                                  p["conv_ln_b"][0], b, t)
    x = _res_matmul([y], wb["conv_pw2"], x, g_m, t, tm)
    y_out = _mlp(x, p["norm_mlp"][1], sc_f, sh_f, wb["mlp_up"], wb["mlp_down"], 1, g_f, t, tm_mlp,
                 final_g=p["final_g"])

    h_a, h_b = mix_a // (2 * D_HEAD), mix_b // D_HEAD
    return (y_out.reshape(b, t, d),
            ka.reshape(1, b, t, 2 * h_a, D_HEAD), va.reshape(1, b, t, h_a, 2 * D_HEAD),
            kb.reshape(1, b, t, h_b, D_HEAD), vb.reshape(1, b, t, h_b, D_HEAD),
            new_state[None])


def kernel(x_prompt, x_sample, c_prompt, c_sample, cache_k_diff, cache_v_diff, cache_k_sb, cache_v_sb, state_conv, w_mod, b_mod, norm_mix, norm_mlp, w_attn_in, w_attn_out, lambda_q1, lambda_k1, lambda_q2, lambda_k2, diff_subln_g, conv_pw1, conv_dw, conv_dw_b, conv_ln_g, conv_ln_b, conv_pw2, mlp_up, mlp_down, final_g):
    assert w_mod.shape[0] == 2 and w_attn_in.shape[0] == 1 and conv_pw1.shape[0] == 1
    p = dict(norm_mix=norm_mix, norm_mlp=norm_mlp, lambda_q1=lambda_q1, lambda_k1=lambda_k1,
             lambda_q2=lambda_q2, lambda_k2=lambda_k2, diff_subln_g=diff_subln_g, conv_dw=conv_dw,
             conv_dw_b=conv_dw_b, conv_ln_g=conv_ln_g, conv_ln_b=conv_ln_b, final_g=final_g)
    wb = dict(w_attn_in=w_attn_in[0].astype(BF16), w_attn_out=w_attn_out[0].astype(BF16),
              conv_pw1=conv_pw1[0].astype(BF16), conv_pw2=conv_pw2[0].astype(BF16),
              mlp_up=mlp_up.astype(BF16), mlp_down=mlp_down.astype(BF16))
    nb = c_prompt.shape[0]
    mod = _modulation(jnp.concatenate([c_prompt, c_sample], axis=0), w_mod, b_mod)
    out_p = _trunk(x_prompt, mod[:, :nb], None, None, p, wb)
    cache = (cache_k_diff[0], cache_v_diff[0], cache_k_sb[0], cache_v_sb[0])
    out_s = _trunk(x_sample, mod[:, nb:], cache, state_conv[0], p, wb)
    return (out_p[0], out_s[0]) + out_p[1:] + out_s[1:]
```
